```python
import math
import jax, jax.numpy as jnp
from jax import lax
import numpy as np

D_MODEL = 2048
BATCH = 8
SEQ = 2048
DEPTH = 1

ATTN_WIDTH = D_MODEL // 2
SSM_WIDTH = D_MODEL // 2
HEAD_DIM = 64
N_Q_HEADS = ATTN_WIDTH // HEAD_DIM
N_KV_HEADS = N_Q_HEADS // 4
KV_WIDTH = N_KV_HEADS * HEAD_DIM
WINDOW = 128
BLOCK = WINDOW
ROPE_THETA = 500000.0
ROPE_DIMS = HEAD_DIM // 4
SSM_GROUP = 16
N_SSM_GROUPS = SSM_WIDTH // SSM_GROUP
SSM_STATE = 64
DT_MIN = 1e-3
DT_MAX = 1e-1
N_BRANCHES = 2
IN_COLS = ATTN_WIDTH + 2 * KV_WIDTH + SSM_WIDTH + N_BRANCHES * D_MODEL
PEER_HEADS = 8
PEER_NKEYS = 128
PEER_N = PEER_NKEYS * PEER_NKEYS
PEER_QDIM = 256
PEER_HALF = PEER_QDIM // 2
PEER_TOPK = 16
PEER_CHUNK = 128
RMS_EPS = 1e-6
MASK_VALUE = -1e30

kernel_name = 'hybrid_swa_s5_peer_adaln_block'


def rmsnorm(x, g):
    xf = x.astype(jnp.float32)
    y = xf * lax.rsqrt(jnp.mean(xf * xf, axis=-1, keepdims=True) + RMS_EPS)
    return (y * g.astype(jnp.float32)).astype(x.dtype)


def modulate(h, shift, scale):
    return h * (1.0 + scale[:, None, :]) + shift[:, None, :]


def rope_partial(t, positions):
    inv_freq = ROPE_THETA ** (-(jnp.arange(0, ROPE_DIMS, 2, dtype=jnp.float32) / ROPE_DIMS))
    ang = positions.astype(jnp.float32)[..., None] * inv_freq
    cos = jnp.cos(ang)[:, :, None, :]
    sin = jnp.sin(ang)[:, :, None, :]
    tr = t[..., :ROPE_DIMS].astype(jnp.float32)
    t1, t2 = tr[..., :ROPE_DIMS // 2], tr[..., ROPE_DIMS // 2:]
    rot = jnp.concatenate([t1 * cos - t2 * sin, t2 * cos + t1 * sin], axis=-1)
    return jnp.concatenate([rot.astype(t.dtype), t[..., ROPE_DIMS:]], axis=-1)


def sliding_window_attention(q, k, v, sinks):
    b, s = q.shape[0], q.shape[1]
    nb = s // BLOCK
    rep = N_Q_HEADS // N_KV_HEADS
    qb = q.reshape(b, nb, BLOCK, N_KV_HEADS, rep, HEAD_DIM)

    def band(t):
        tp = jnp.pad(t, ((0, 0), (BLOCK, 0), (0, 0), (0, 0)))
        cur = tp[:, BLOCK:].reshape(b, nb, BLOCK, N_KV_HEADS, HEAD_DIM)
        prev = tp[:, :-BLOCK].reshape(b, nb, BLOCK, N_KV_HEADS, HEAD_DIM)
        return jnp.concatenate([prev, cur], axis=2)

    kb, vb = band(k), band(v)
    scores = jnp.einsum('bnqgrd,bnkgd->bngrqk', qb, kb,
                        preferred_element_type=jnp.float32) * (HEAD_DIM ** -0.5)
    qi = jnp.arange(BLOCK)[:, None] + BLOCK
    kj = jnp.arange(2 * BLOCK)[None, :]
    diff = qi - kj
    valid = (diff >= 0) & (diff < WINDOW)
    blk = jnp.arange(nb)[:, None, None]
    valid = valid[None] & ~((blk == 0) & (kj[None] < BLOCK))
    scores = jnp.where(valid[None, :, None, None], scores, MASK_VALUE)
    sink = sinks.astype(jnp.float32).reshape(N_KV_HEADS, rep)[None, None, :, :, None, None]
    sink = jnp.broadcast_to(sink, scores.shape[:-1] + (1,))
    probs = jax.nn.softmax(jnp.concatenate([scores, sink], axis=-1), axis=-1)[..., :-1]
    out = jnp.einsum('bngrqk,bnkgd->bnqgrd', probs.astype(v.dtype), vb)
    return out.reshape(b, s, N_Q_HEADS * HEAD_DIM)


def s5_ssm(u, A_re, A_im, log_dt, B_re, B_im, C_re, C_im, D_skip, w_glu, b_glu):
    b, s, _ = u.shape
    uf = u.astype(jnp.float32).reshape(b, s, N_SSM_GROUPS, SSM_GROUP)
    dt = jnp.exp(log_dt.astype(jnp.float32))[:, None]
    lam_re = jnp.minimum(A_re.astype(jnp.float32), -1e-4)
    lam_im = A_im.astype(jnp.float32)
    mag = jnp.exp(lam_re * dt)
    ab_re = mag * jnp.cos(lam_im * dt)
    ab_im = mag * jnp.sin(lam_im * dt)
    den = lam_re * lam_re + lam_im * lam_im
    num_re = ab_re - 1.0
    z_re = (num_re * lam_re + ab_im * lam_im) / den
    z_im = (ab_im * lam_re - num_re * lam_im) / den
    bre = B_re.astype(jnp.float32)
    bim = B_im.astype(jnp.float32)
    bb_re = z_re[..., None] * bre - z_im[..., None] * bim
    bb_im = z_re[..., None] * bim + z_im[..., None] * bre
    bu_re = jnp.einsum('bsgh,gph->bsgp', uf, bb_re)
    bu_im = jnp.einsum('bsgh,gph->bsgp', uf, bb_im)
    a_re = jnp.broadcast_to(ab_re[None, None], (1, s, N_SSM_GROUPS, SSM_STATE))
    a_im = jnp.broadcast_to(ab_im[None, None], (1, s, N_SSM_GROUPS, SSM_STATE))

    def combine(left, right):
        a1r, a1i, b1r, b1i = left
        a2r, a2i, b2r, b2i = right
        return (a2r * a1r - a2i * a1i,
                a2r * a1i + a2i * a1r,
                a2r * b1r - a2i * b1i + b2r,
                a2r * b1i + a2i * b1r + b2i)

    _, _, x_re, x_im = lax.associative_scan(combine, (a_re, a_im, bu_re, bu_im), axis=1)
    y = (jnp.einsum('bsgp,ghp->bsgh', x_re, C_re.astype(jnp.float32))
         - jnp.einsum('bsgp,ghp->bsgh', x_im, C_im.astype(jnp.float32))
         + D_skip.astype(jnp.float32).reshape(N_SSM_GROUPS, SSM_GROUP) * uf)
    y = y.reshape(b, s, SSM_WIDTH)
    z = jax.nn.gelu(y)
    out = z * jax.nn.sigmoid(z @ w_glu.astype(jnp.float32) + b_glu.astype(jnp.float32))
    return out.astype(u.dtype)


def peer(h, w_query, sub_keys, expert_down, expert_up):
    b, s, d = h.shape
    t = b * s
    hf = h.reshape(t, d)
    q = (hf @ w_query).reshape(t, PEER_HEADS, 2, PEER_HALF)
    sc = jnp.einsum('thcd,hckd->thck', q, sub_keys, preferred_element_type=jnp.float32)
    s_top, i_top = lax.top_k(sc, PEER_TOPK)
    cand = s_top[:, :, 0, :, None] + s_top[:, :, 1, None, :]
    cand_idx = i_top[:, :, 0, :, None] * PEER_NKEYS + i_top[:, :, 1, None, :]
    best, sel = lax.top_k(cand.reshape(t, PEER_HEADS, PEER_TOPK * PEER_TOPK), PEER_TOPK)
    idx = jnp.take_along_axis(cand_idx.reshape(t, PEER_HEADS, PEER_TOPK * PEER_TOPK), sel, axis=-1)
    gates = jax.nn.softmax(best, axis=-1)

    def chunk_fn(args):
        xc, ic, gc = args
        u = jnp.take(expert_down, ic, axis=0)
        a = jnp.einsum('td,thkd->thk', xc, u)
        w = jax.nn.gelu(a.astype(jnp.float32)) * gc
        vv = jnp.take(expert_up, ic, axis=0)
        return jnp.einsum('thk,thkd->td', w.astype(xc.dtype), vv)

    nc = t // PEER_CHUNK
    out = lax.map(chunk_fn, (hf.reshape(nc, PEER_CHUNK, d),
                             idx.reshape(nc, PEER_CHUNK, PEER_HEADS, PEER_TOPK),
                             gates.reshape(nc, PEER_CHUNK, PEER_HEADS, PEER_TOPK)))
    return out.reshape(b, s, d)


def hybrid_layer(x, c, positions, w_ada, b_ada, g_mix, w_in, b_in, attn_sinks, w_attn_branch,
                 ssm_A_re, ssm_A_im, ssm_log_dt, ssm_B_re, ssm_B_im, ssm_C_re, ssm_C_im, ssm_D,
                 w_glu, b_glu, w_ssm_branch, w_out, g_ffn, w_query, sub_keys, expert_down, expert_up):
    b, s, _ = x.shape
    mod = jax.nn.silu(c) @ w_ada + b_ada
    shift1, scale1, gate1, shift2, scale2, gate2 = jnp.split(mod, 6, axis=-1)

    h = modulate(rmsnorm(x, g_mix), shift1, scale1)
    proj = h @ w_in + b_in
    o1 = ATTN_WIDTH
    o2 = o1 + KV_WIDTH
    o3 = o2 + KV_WIDTH
    o4 = o3 + SSM_WIDTH
    q, k, v, u, gate_logits = jnp.split(proj, [o1, o2, o3, o4], axis=-1)
    q = rope_partial(q.reshape(b, s, N_Q_HEADS, HEAD_DIM), positions)
    k = rope_partial(k.reshape(b, s, N_KV_HEADS, HEAD_DIM), positions)
    v = v.reshape(b, s, N_KV_HEADS, HEAD_DIM)
    y_attn = sliding_window_attention(q, k, v, attn_sinks) @ w_attn_branch
    y_ssm = s5_ssm(u, ssm_A_re, ssm_A_im, ssm_log_dt, ssm_B_re, ssm_B_im, ssm_C_re, ssm_C_im,
                   ssm_D, w_glu, b_glu) @ w_ssm_branch
    g = jax.nn.sigmoid(gate_logits.reshape(b, s, N_BRANCHES, D_MODEL))
    merged = g[:, :, 0] * y_attn + g[:, :, 1] * y_ssm
    x = x + gate1[:, None, :] * (merged @ w_out)

    h2 = modulate(rmsnorm(x, g_ffn), shift2, scale2)
    x = x + gate2[:, None, :] * peer(h2, w_query, sub_keys, expert_down, expert_up)
    return x


def setup_inputs(seed: int = 0) -> dict:
    key = jax.random.key(seed)
    ks = jax.random.split(key, 32)
    f32 = jnp.float32

    def nrm(k, shape, scale):
        return jax.random.normal(k, shape, f32) * scale

    L = DEPTH
    x = nrm(ks[0], (BATCH, SEQ, D_MODEL), 1.0)
    c = nrm(ks[1], (BATCH, D_MODEL), 1.0)
    offset = jax.random.randint(ks[2], (BATCH,), 0, 4096, dtype=jnp.int32)
    positions = offset[:, None] + jnp.arange(SEQ, dtype=jnp.int32)[None, :]
    n_idx = jnp.arange(SSM_STATE, dtype=f32)
    return {
        'x': x,
        'c': c,
        'positions': positions,
        'w_ada': nrm(ks[3], (L, D_MODEL, 6 * D_MODEL), 0.5 * D_MODEL ** -0.5),
        'b_ada': nrm(ks[4], (L, 6 * D_MODEL), 0.02),
        'g_mix': 1.0 + nrm(ks[5], (L, D_MODEL), 0.02),
        'w_in': nrm(ks[6], (L, D_MODEL, IN_COLS), D_MODEL ** -0.5),
        'b_in': nrm(ks[7], (L, IN_COLS), 0.02),
        'attn_sinks': nrm(ks[8], (L, N_Q_HEADS), 0.5),
        'w_attn_branch': nrm(ks[9], (L, ATTN_WIDTH, D_MODEL), ATTN_WIDTH ** -0.5),
        'ssm_A_re': -0.5 + nrm(ks[10], (L, N_SSM_GROUPS, SSM_STATE), 0.01),
        'ssm_A_im': math.pi * n_idx + nrm(ks[11], (L, N_SSM_GROUPS, SSM_STATE), 0.01),
        'ssm_log_dt': jax.random.uniform(ks[12], (L, N_SSM_GROUPS), f32, math.log(DT_MIN), math.log(DT_MAX)),
        'ssm_B_re': nrm(ks[13], (L, N_SSM_GROUPS, SSM_STATE, SSM_GROUP), (2.0 * SSM_GROUP) ** -0.5),
        'ssm_B_im': nrm(ks[14], (L, N_SSM_GROUPS, SSM_STATE, SSM_GROUP), (2.0 * SSM_GROUP) ** -0.5),
        'ssm_C_re': nrm(ks[15], (L, N_SSM_GROUPS, SSM_GROUP, SSM_STATE), (2.0 * SSM_STATE) ** -0.5),
        'ssm_C_im': nrm(ks[16], (L, N_SSM_GROUPS, SSM_GROUP, SSM_STATE), (2.0 * SSM_STATE) ** -0.5),
        'ssm_D': nrm(ks[17], (L, SSM_WIDTH), 1.0),
        'w_glu': nrm(ks[18], (L, SSM_WIDTH, SSM_WIDTH), SSM_WIDTH ** -0.5),
        'b_glu': nrm(ks[19], (L, SSM_WIDTH), 0.02),
        'w_ssm_branch': nrm(ks[20], (L, SSM_WIDTH, D_MODEL), SSM_WIDTH ** -0.5),
        'w_out': nrm(ks[21], (L, D_MODEL, D_MODEL), D_MODEL ** -0.5),
        'g_ffn': 1.0 + nrm(ks[22], (L, D_MODEL), 0.02),
        'w_query': nrm(ks[23], (L, D_MODEL, PEER_HEADS * PEER_QDIM), D_MODEL ** -0.5),
        'sub_keys': nrm(ks[24], (L, PEER_HEADS, 2, PEER_NKEYS, PEER_HALF), PEER_HALF ** -0.5),
        'expert_down': nrm(ks[25], (L, PEER_N, D_MODEL), D_MODEL ** -0.5),
        'expert_up': nrm(ks[26], (L, PEER_N, D_MODEL), 0.5),
        'g_final': 1.0 + nrm(ks[27], (D_MODEL,), 0.02),
    }


def reference(x, c, positions, w_ada, b_ada, g_mix, w_in, b_in, attn_sinks, w_attn_branch,
              ssm_A_re, ssm_A_im, ssm_log_dt, ssm_B_re, ssm_B_im, ssm_C_re, ssm_C_im, ssm_D,
              w_glu, b_glu, w_ssm_branch, w_out, g_ffn, w_query, sub_keys, expert_down, expert_up,
              g_final):
    for layer in range(DEPTH):
        x = hybrid_layer(x, c, positions, w_ada[layer], b_ada[layer], g_mix[layer], w_in[layer],
                         b_in[layer], attn_sinks[layer], w_attn_branch[layer], ssm_A_re[layer],
                         ssm_A_im[layer], ssm_log_dt[layer], ssm_B_re[layer], ssm_B_im[layer],
                         ssm_C_re[layer], ssm_C_im[layer], ssm_D[layer], w_glu[layer], b_glu[layer],
                         w_ssm_branch[layer], w_out[layer], g_ffn[layer], w_query[layer],
                         sub_keys[layer], expert_down[layer], expert_up[layer])
    return rmsnorm(x, g_final)
```

```python
import functools
import math

import jax
import jax.numpy as jnp
from jax import lax
from jax.experimental import pallas as pl
from jax.experimental.pallas import tpu as pltpu

F32 = jnp.float32
BF16 = jnp.bfloat16

D_MODEL = 2048
ATTN_WIDTH = 1024
SSM_WIDTH = 1024
HEAD_DIM = 64
N_Q_HEADS = 16
N_KV_HEADS = 4
KV_WIDTH = 256
WINDOW = 128
ROPE_THETA = 500000.0
ROPE_DIMS = 16
SSM_GROUP = 16
N_SSM_GROUPS = 64
SSM_STATE = 64
STATE_LANES = N_SSM_GROUPS * SSM_STATE
PEER_HEADS = 8
PEER_NKEYS = 128
PEER_N = PEER_NKEYS * PEER_NKEYS
PEER_HALF = 128
PEER_TOPK = 16
RMS_EPS = 1e-6
MASK_VALUE = -1e30
IN_COLS = ATTN_WIDTH + 2 * KV_WIDTH + SSM_WIDTH + 2 * D_MODEL

LANES = 128
SSM_SLICE_GROUPS = 8
VMEM_LIMIT = 56 * 1024 * 1024

COL_Q = 0
COL_U = ATTN_WIDTH
COL_GA = COL_U + SSM_WIDTH
COL_GS = COL_GA + D_MODEL
COL_KV = COL_GS + D_MODEL


def _cparams(sem):
    return pltpu.CompilerParams(dimension_semantics=sem, vmem_limit_bytes=VMEM_LIMIT)


def _rms_mod(x, g, scale, shift):
    ms = jnp.mean(x * x, axis=-1, keepdims=True)
    y = x * lax.rsqrt(ms + RMS_EPS) * g
    return y * (1.0 + scale) + shift


def _ada_kernel(c_ref, w_ref, b_ref, o_ref):
    c = c_ref[...]
    sc = c * jax.nn.sigmoid(c)
    o_ref[...] = jnp.dot(sc, w_ref[...], preferred_element_type=F32,
                         precision=lax.Precision.HIGHEST) + b_ref[...]


def _ada_mod(c, w_ada, b_ada):
    bsz, d = c.shape
    n = w_ada.shape[1]
    tn = 1024
    return pl.pallas_call(
        _ada_kernel,
        grid=(n // tn,),
        in_specs=[pl.BlockSpec((bsz, d), lambda j: (0, 0)),
                  pl.BlockSpec((d, tn), lambda j: (0, j)),
                  pl.BlockSpec((1, tn), lambda j: (0, j))],
        out_specs=pl.BlockSpec((bsz, tn), lambda j: (0, j)),
        out_shape=jax.ShapeDtypeStruct((bsz, n), F32),
        compiler_params=_cparams(("parallel",)),
        name="ada_mod",
    )(c, w_ada, b_ada.reshape(1, n))


def _inproj_kernel(x_ref, g_ref, sc_ref, sh_ref, w_ref, b_ref, o_ref, h_ref):
    @pl.when(pl.program_id(2) == 0)
    def _():
        h_ref[...] = _rms_mod(x_ref[...], g_ref[...], sc_ref[...], sh_ref[...]).astype(BF16)

    o_ref[...] = jnp.dot(h_ref[...], w_ref[...], preferred_element_type=F32) + b_ref[...]


def _in_proj(x, g_mix, scale1, shift1, w_in_p, b_in_p):
    bsz, s, d = x.shape
    n = w_in_p.shape[1]
    tm = min(1024, s)
    tn = 512
    return pl.pallas_call(
        _inproj_kernel,
        grid=(bsz, s // tm, n // tn),
        in_specs=[pl.BlockSpec((None, tm, d), lambda b, i, j: (b, i, 0)),
                  pl.BlockSpec((1, d), lambda b, i, j: (0, 0)),
                  pl.BlockSpec((None, 1, d), lambda b, i, j: (b, 0, 0)),
                  pl.BlockSpec((None, 1, d), lambda b, i, j: (b, 0, 0)),
                  pl.BlockSpec((d, tn), lambda b, i, j: (0, j)),
                  pl.BlockSpec((1, tn), lambda b, i, j: (0, j))],
        out_specs=pl.BlockSpec((None, tm, tn), lambda b, i, j: (b, i, j)),
        out_shape=jax.ShapeDtypeStruct((bsz, s, n), F32),
        scratch_shapes=[pltpu.VMEM((tm, d), BF16)],
        compiler_params=_cparams(("parallel", "parallel", "arbitrary")),
        name="in_proj",
    )(x, g_mix.reshape(1, d), scale1, shift1, w_in_p, b_in_p.reshape(1, n))


def _rope_kernel(pos_ref, freq_ref, c_ref, s1_ref, s2_ref):
    ang = pos_ref[...] * freq_ref[...]
    cs = jnp.cos(ang)
    sn = jnp.sin(ang)
    d = lax.broadcasted_iota(jnp.int32, ang.shape, 1) & (HEAD_DIM - 1)
    c_ref[...] = cs
    s1_ref[...] = jnp.where(d < ROPE_DIMS // 2, -sn, 0.0)
    s2_ref[...] = jnp.where((d >= ROPE_DIMS // 2) & (d < ROPE_DIMS), sn, 0.0)


def _rope_tables(positions):
    bsz, s = positions.shape
    t = bsz * s
    tm = min(1024, t)
    inv_freq = ROPE_THETA ** (-(jnp.arange(0, ROPE_DIMS, 2, dtype=F32) / ROPE_DIMS))
    d = jnp.arange(LANES) % HEAD_DIM
    freq = jnp.where(d < ROPE_DIMS, inv_freq[d % (ROPE_DIMS // 2)], 0.0).astype(F32).reshape(1, LANES)
    pos = positions.astype(F32).reshape(t, 1)
    sds = jax.ShapeDtypeStruct((t, LANES), F32)
    outs = pl.pallas_call(
        _rope_kernel,
        grid=(t // tm,),
        in_specs=[pl.BlockSpec((tm, 1), lambda i: (i, 0)),
                  pl.BlockSpec((1, LANES), lambda i: (0, 0))],
        out_specs=[pl.BlockSpec((tm, LANES), lambda i: (i, 0))] * 3,
        out_shape=[sds, sds, sds],
        compiler_params=_cparams(("parallel",)),
        name="rope_tables",
    )(pos, freq)
    return [o.reshape(bsz, s, LANES) for o in outs]


def _rope_apply(t, c, s1, s2):
    outs = []
    for j in range(t.shape[1] // LANES):
        tj = t[:, j * LANES:(j + 1) * LANES]
        outs.append(tj * c + pltpu.roll(tj, LANES - ROPE_DIMS // 2, 1) * s1
                    + pltpu.roll(tj, ROPE_DIMS // 2, 1) * s2)
    return jnp.concatenate(outs, axis=1)


def _attn_kernel(sink_ref, q_ref, kvc_ref, kvp_ref, cq_ref, s1q_ref, s2q_ref,
                 cp_ref, s1p_ref, s2p_ref, bm_ref, o_ref):
    n = pl.program_id(1)
    cq, s1q, s2q = cq_ref[...], s1q_ref[...], s2q_ref[...]
    q = (_rope_apply(q_ref[...], cq, s1q, s2q) * (HEAD_DIM ** -0.5)).astype(BF16)
    kvc = kvc_ref[...]
    kvp = kvp_ref[...]
    kc = _rope_apply(kvc[:, :KV_WIDTH], cq, s1q, s2q)
    kp = _rope_apply(kvp[:, :KV_WIDTH], cp_ref[...], s1p_ref[...], s2p_ref[...])
    kband = jnp.concatenate([kp, kc], axis=0).astype(BF16)
    vband = jnp.concatenate([kvp[:, KV_WIDTH:], kvc[:, KV_WIDTH:]], axis=0).astype(BF16)
    bm = bm_ref[...]
    kexp = jnp.concatenate([kband] * N_KV_HEADS, axis=0) * bm
    vexp = jnp.concatenate([vband] * N_KV_HEADS, axis=0) * bm
    row = lax.broadcasted_iota(jnp.int32, (WINDOW, 2 * WINDOW), 0)
    col = lax.broadcasted_iota(jnp.int32, (WINDOW, 2 * WINDOW), 1)
    diff = row + WINDOW - col
    valid = (diff >= 0) & (diff < WINDOW) & ((col >= WINDOW) | (n > 0))
    rep = N_Q_HEADS // N_KV_HEADS
    kw = 2 * WINDOW
    for r in range(rep):
        qr = q[:, r * KV_WIDTH:(r + 1) * KV_WIDTH]
        sc = lax.dot_general(qr, kexp, (((1,), (1,)), ((), ())), preferred_element_type=F32)
        ps = []
        for g in range(N_KV_HEADS):
            s = jnp.where(valid, sc[:, g * kw:(g + 1) * kw], MASK_VALUE)
            sink = sink_ref[g * rep + r]
            m = jnp.maximum(jnp.max(s, axis=-1, keepdims=True), sink)
            p = jnp.exp(s - m)
            l = jnp.sum(p, axis=-1, keepdims=True) + jnp.exp(sink - m)
            ps.append((p / l).astype(BF16))
        pcat = jnp.concatenate(ps, axis=1)
        o = jnp.dot(pcat, vexp, preferred_element_type=F32)
        o_ref[:, r * KV_WIDTH:(r + 1) * KV_WIDTH] = o.astype(BF16)


def _attention(proj, sinks, ctab, s1tab, s2tab):
    bsz, s, _ = proj.shape
    nb = s // WINDOW
    g_row = jnp.arange(N_KV_HEADS * 2 * WINDOW) // (2 * WINDOW)
    g_col = jnp.arange(KV_WIDTH) // HEAD_DIM
    bm = (g_row[:, None] == g_col[None, :]).astype(BF16)
    cur = lambda b, n: (b, n, 0)
    prev = lambda b, n: (b, jnp.maximum(n - 1, 0), 0)
    kvw = 2 * KV_WIDTH
    tab = pl.BlockSpec((None, WINDOW, LANES), cur)
    tabp = pl.BlockSpec((None, WINDOW, LANES), prev)
    return pl.pallas_call(
        _attn_kernel,
        grid=(bsz, nb),
        in_specs=[pl.BlockSpec(memory_space=pltpu.SMEM),
                  pl.BlockSpec((None, WINDOW, ATTN_WIDTH), cur),
                  pl.BlockSpec((None, WINDOW, kvw), lambda b, n: (b, n, COL_KV // kvw)),
                  pl.BlockSpec((None, WINDOW, kvw), lambda b, n: (b, jnp.maximum(n - 1, 0), COL_KV // kvw)),
                  tab, tab, tab, tabp, tabp, tabp,
                  pl.BlockSpec((N_KV_HEADS * 2 * WINDOW, KV_WIDTH), lambda b, n: (0, 0))],
        out_specs=pl.BlockSpec((None, WINDOW, ATTN_WIDTH), cur),
        out_shape=jax.ShapeDtypeStruct((bsz, s, ATTN_WIDTH), BF16),
        compiler_params=_cparams(("parallel", "parallel")),
        name="attention",
    )(sinks, proj, proj, proj, ctab, s1tab, s2tab, ctab, s1tab, s2tab, bm)


def _ssm_prep_kernel(are_ref, aim_ref, ldt_ref, bre_ref, bim_ref,
                     abre_ref, abim_ref, bbre_ref, bbim_ref):
    dt = jnp.exp(ldt_ref[...])
    lam_re = jnp.minimum(are_ref[...], -1e-4)
    lam_im = aim_ref[...]
    mag = jnp.exp(lam_re * dt)
    ab_re = mag * jnp.cos(lam_im * dt)
    ab_im = mag * jnp.sin(lam_im * dt)
    den = lam_re * lam_re + lam_im * lam_im
    num_re = ab_re - 1.0
    z_re = (num_re * lam_re + ab_im * lam_im) / den
    z_im = (ab_im * lam_re - num_re * lam_im) / den
    abre_ref[...] = ab_re
    abim_ref[...] = ab_im
    bre = bre_ref[...]
    bim = bim_ref[...]
    bbre_ref[...] = z_re * bre - z_im * bim
    bbim_ref[...] = z_re * bim + z_im * bre


def _ssm_prep(a_re, a_im, log_dt, b_re, b_im):
    g, p, h = b_re.shape
    a3 = jax.ShapeDtypeStruct((g, 1, p), F32)
    b3 = jax.ShapeDtypeStruct((g, h, p), F32)
    return pl.pallas_call(
        _ssm_prep_kernel,
        out_shape=[a3, a3, b3, b3],
        name="ssm_prep",
    )(a_re.reshape(g, 1, p), a_im.reshape(g, 1, p), log_dt.reshape(g, 1, 1),
      jnp.swapaxes(b_re, 1, 2), jnp.swapaxes(b_im, 1, 2))


def _ssm_kernel(u_ref, wb_ref, wc_ref, are_ref, aim_ref, d_ref, wg_ref, bg_ref, o_ref,
                xre, xim, sre, sim, *, n_batch, t_chunk, lane_chunk):
    @pl.when(pl.program_id(0) == 0)
    def _():
        sre[...] = jnp.zeros_like(sre)
        sim[...] = jnp.zeros_like(sim)

    n_slices = SSM_WIDTH // LANES
    sw = SSM_SLICE_GROUPS * SSM_STATE
    u = u_ref[...]
    ub = u.astype(BF16)
    for s in range(n_slices):
        bu = jnp.dot(ub[:, s * LANES:(s + 1) * LANES], wb_ref[s], preferred_element_type=F32)
        xre[:, s * sw:(s + 1) * sw] = bu[:, :sw]
        xim[:, s * sw:(s + 1) * sw] = bu[:, sw:]

    for lc in range(STATE_LANES // lane_chunk):
        sl = slice(lc * lane_chunk, (lc + 1) * lane_chunk)
        ar = jnp.broadcast_to(are_ref[:, sl], (n_batch, lane_chunk))
        ai = jnp.broadcast_to(aim_ref[:, sl], (n_batch, lane_chunk))

        def body(t, carry, sl=sl, ar=ar, ai=ai):
            cr, ci = carry
            off = pl.multiple_of(t * n_batch, n_batch)
            nr = ar * cr - ai * ci + xre[pl.ds(off, n_batch), sl]
            ni = ar * ci + ai * cr + xim[pl.ds(off, n_batch), sl]
            xre[pl.ds(off, n_batch), sl] = nr
            xim[pl.ds(off, n_batch), sl] = ni
            return nr, ni

        cr, ci = lax.fori_loop(0, t_chunk, body, (sre[:, sl], sim[:, sl]), unroll=8)
        sre[:, sl] = cr
        sim[:, sl] = ci

    ys = []
    for s in range(n_slices):
        xs = jnp.concatenate([xre[:, s * sw:(s + 1) * sw], xim[:, s * sw:(s + 1) * sw]], axis=1)
        ys.append(jnp.dot(xs.astype(BF16), wc_ref[s], preferred_element_type=F32))
    y = jnp.concatenate(ys, axis=1) + d_ref[...] * u
    z = jax.nn.gelu(y)
    gl = jnp.dot(z.astype(BF16), wg_ref[...], preferred_element_type=F32) + bg_ref[...]
    o_ref[...] = (z * jax.nn.sigmoid(gl)).astype(BF16)


def _ssm(u_tm, n_batch, wb, wc, ab_re, ab_im, d_skip, w_glu, b_glu):
    rows, w = u_tm.shape
    s = rows // n_batch
    t_chunk = min(64, s)
    r = t_chunk * n_batch
    full = lambda shape: pl.BlockSpec(shape, lambda i: (0,) * len(shape))
    kern = functools.partial(_ssm_kernel, n_batch=n_batch, t_chunk=t_chunk, lane_chunk=512)
    return pl.pallas_call(
        kern,
        grid=(s // t_chunk,),
        in_specs=[pl.BlockSpec((r, w), lambda i: (i, 0)),
                  full(wb.shape), full(wc.shape), full((1, STATE_LANES)), full((1, STATE_LANES)),
                  full((1, w)), full((w, w)), full((1, w))],
        out_specs=pl.BlockSpec((r, w), lambda i: (i, 0)),
        out_shape=jax.ShapeDtypeStruct((rows, w), BF16),
        scratch_shapes=[pltpu.VMEM((r, STATE_LANES), F32), pltpu.VMEM((r, STATE_LANES), F32),
                        pltpu.VMEM((n_batch, STATE_LANES), F32), pltpu.VMEM((n_batch, STATE_LANES), F32)],
        compiler_params=_cparams(("arbitrary",)),
        name="ssm",
    )(u_tm, wb, wc, ab_re, ab_im, d_skip.reshape(1, w), w_glu, b_glu.reshape(1, w))


def _merge_kernel(attn_ref, sso_ref, ga_ref, gs_ref, x_ref, g1_ref, gf_ref, sc2_ref, sh2_ref,
                  wab_ref, wsb_ref, wo_ref, x1_ref, hi_ref, lo_ref):
    ya = jnp.dot(attn_ref[...], wab_ref[...], preferred_element_type=F32)
    ys = jnp.dot(sso_ref[...], wsb_ref[...], preferred_element_type=F32)
    merged = jax.nn.sigmoid(ga_ref[...]) * ya + jax.nn.sigmoid(gs_ref[...]) * ys
    mo = jnp.dot(merged.astype(BF16), wo_ref[...], preferred_element_type=F32)
    x1 = x_ref[...] + g1_ref[...] * mo
    x1_ref[...] = x1
    h2t = _rms_mod(x1, gf_ref[...], sc2_ref[...], sh2_ref[...]).T
    hi = h2t.astype(BF16)
    hi_ref[...] = hi
    lo_ref[...] = (h2t - hi.astype(F32)).astype(BF16)


def _merge(attn, sso, proj, x, gate1, g_ffn, scale2, shift2, wab, wsb, wo):
    bsz, s, d = x.shape
    tm = min(256, s)
    nt = s // tm
    row = lambda w, cb: pl.BlockSpec((None, tm, w), lambda b, i: (b, i, cb))
    per_b = pl.BlockSpec((None, 1, d), lambda b, i: (b, 0, 0))
    const = lambda shape: pl.BlockSpec(shape, lambda b, i: (0, 0), pipeline_mode=pl.Buffered(1))
    tsds = jax.ShapeDtypeStruct((d, bsz * s), BF16)
    tspec = pl.BlockSpec((d, tm), lambda b, i: (0, b * nt + i))
    return pl.pallas_call(
        _merge_kernel,
        grid=(bsz, nt),
        in_specs=[row(ATTN_WIDTH, 0), row(SSM_WIDTH, 0), row(d, COL_GA // d), row(d, COL_GS // d),
                  row(d, 0), per_b, pl.BlockSpec((1, d), lambda b, i: (0, 0)), per_b, per_b,
                  const(wab.shape), const(wsb.shape), const(wo.shape)],
        out_specs=[row(d, 0), tspec, tspec],
        out_shape=[jax.ShapeDtypeStruct((bsz, s, d), F32), tsds, tsds],
        compiler_params=_cparams(("parallel", "parallel")),
        name="merge",
    )(attn, sso, proj, proj, x, gate1, g_ffn.reshape(1, d), scale2, shift2, wab, wsb, wo)


def _dot3(a_hi, a_lo, b_hi, b_lo):
    return (jnp.dot(a_hi, b_hi, preferred_element_type=F32)
            + jnp.dot(a_hi, b_lo, preferred_element_type=F32)
            + jnp.dot(a_lo, b_hi, preferred_element_type=F32))


def _top16(v):
    idx = lax.broadcasted_iota(jnp.int32, v.shape, 0)
    rank = jnp.full(v.shape, 100.0, F32)
    vals = []
    for it in range(PEER_TOPK):
        m = jnp.max(v, axis=0, keepdims=True)
        istar = jnp.min(jnp.where(v == m, idx, PEER_NKEYS), axis=0, keepdims=True)
        sel = idx == istar
        rank = jnp.where(sel, float(it + 1), rank)
        v = jnp.where(sel, -jnp.inf, v)
        vals.append(m)
    return jnp.concatenate(vals, axis=0), rank


def _staircase(a, b):
    lanes = a.shape[1]
    rowi = lax.broadcasted_iota(jnp.int32, (8, lanes), 0)
    cs, codes = [], []
    for q in range(PEER_TOPK):
        nq = PEER_TOPK // (q + 1)
        for blk in range((nq + 7) // 8):
            p = rowi + 8 * blk
            c = a[8 * blk:8 * blk + 8, :] + b[q:q + 1, :]
            cs.append(jnp.where(p < nq, c, -jnp.inf))
            codes.append(p * PEER_TOPK + q)
    cand = jnp.concatenate(cs, axis=0)
    code = jnp.concatenate(codes, axis=0)
    row16 = lax.broadcasted_iota(jnp.int32, (PEER_TOPK, lanes), 0)
    cnt = jnp.zeros((PEER_TOPK, lanes), F32)
    m0 = a[0:1, :] + b[0:1, :]
    z = jnp.zeros((1, lanes), F32)
    for _ in range(PEER_TOPK):
        m = jnp.max(cand, axis=0, keepdims=True)
        cstar = jnp.min(jnp.where(cand == m, code, 1 << 20), axis=0, keepdims=True)
        cand = jnp.where(code == cstar, -jnp.inf, cand)
        z = z + jnp.exp(m - m0)
        cnt = cnt + jnp.where(row16 == (cstar >> 4), 1.0, 0.0)
    return cnt, z


def _route_kernel(xh_ref, xl_ref, wh_ref, wl_ref, kh_ref, kl_ref,
                  g_ref, l_ref, e1_ref, r1_ref, sc0, sc1):
    qt = _dot3(wh_ref[...], wl_ref[...], xh_ref[...], xl_ref[...])
    for c, dst in ((0, sc0), (1, sc1)):
        qc = qt[c * PEER_HALF:(c + 1) * PEER_HALF, :]
        qh = qc.astype(BF16)
        ql = (qc - qh.astype(F32)).astype(BF16)
        dst[...] = _dot3(kh_ref[c], kl_ref[c], qh, ql)

    def lane_group(j, carry):
        off = pl.multiple_of(j * LANES, LANES)
        s0 = sc0[:, pl.ds(off, LANES)]
        s1 = sc1[:, pl.ds(off, LANES)]
        a, rank0 = _top16(s0)
        b, rank1 = _top16(s1)
        cnt, z = _staircase(a, b)
        lsel = jnp.zeros(s0.shape, F32)
        for p in range(PEER_TOPK):
            lsel = lsel + jnp.where(rank0 == float(p + 1), cnt[p:p + 1, :], 0.0)
        g_ref[:, pl.ds(off, LANES)] = jnp.exp(s0 - a[0:1, :]) / z
        l_ref[:, pl.ds(off, LANES)] = lsel
        e1_ref[:, pl.ds(off, LANES)] = jnp.exp(s1 - b[0:1, :]).astype(BF16)
        r1_ref[:, pl.ds(off, LANES)] = rank1.astype(BF16)
        return carry

    lax.fori_loop(0, sc0.shape[1] // LANES, lane_group, 0)


def _peer_route(h2t_hi, h2t_lo, wq_hi, wq_lo, k_hi, k_lo):
    d, t = h2t_hi.shape
    tn = min(512, t)
    qd = 2 * PEER_HALF
    xs = pl.BlockSpec((d, tn), lambda i, h: (0, i))
    ws = pl.BlockSpec((None, qd, d), lambda i, h: (h, 0, 0))
    ks = pl.BlockSpec((None, 2, PEER_NKEYS, PEER_HALF), lambda i, h: (h, 0, 0, 0))
    os_ = pl.BlockSpec((None, PEER_NKEYS, tn), lambda i, h: (h, 0, i))
    f = jax.ShapeDtypeStruct((PEER_HEADS, PEER_NKEYS, t), F32)
    bf = jax.ShapeDtypeStruct((PEER_HEADS, PEER_NKEYS, t), BF16)
    return pl.pallas_call(
        _route_kernel,
        grid=(t // tn, PEER_HEADS),
        in_specs=[xs, xs, ws, ws, ks, ks],
        out_specs=[os_, os_, os_, os_],
        out_shape=[f, f, bf, bf],
        scratch_shapes=[pltpu.VMEM((PEER_NKEYS, tn), F32), pltpu.VMEM((PEER_NKEYS, tn), F32)],
        compiler_params=_cparams(("parallel", "arbitrary")),
        name="peer_route",
    )(h2t_hi, h2t_lo, wq_hi, wq_lo, k_hi, k_lo)


def _dense_kernel(h_ref, dn_ref, up_ref, g_ref, l_ref, e1_ref, r1_ref, o_ref, *, n_i):
    @pl.when(pl.program_id(1) == 0)
    def _():
        o_ref[...] = jnp.zeros_like(o_ref)

    at = jnp.dot(dn_ref[...], h_ref[...], preferred_element_type=F32)
    act = jax.nn.gelu(at).astype(BF16)
    ws = []
    for il in range(n_i):
        w = None
        for h in range(PEER_HEADS):
            gi = g_ref[h, il:il + 1, :].astype(BF16)
            li = l_ref[h, il:il + 1, :].astype(BF16)
            term = jnp.where(r1_ref[h] <= li, e1_ref[h] * gi, jnp.zeros((), BF16))
            w = term if w is None else w + term
        ws.append(w)
    wg = act * jnp.concatenate(ws, axis=0)
    o_ref[...] += jnp.dot(up_ref[...], wg, preferred_element_type=F32)


def _peer_dense(h2t_hi, down_bf, up_t_bf, g, l, e1, r1):
    d, t = h2t_hi.shape
    ne = down_bf.shape[0]
    tm = min(512, t)
    en = 1024
    n_i = en // PEER_NKEYS
    rt = lambda n: pl.BlockSpec((PEER_HEADS, n, tm), lambda i, e: (0, e if n == n_i else 0, i))
    return pl.pallas_call(
        functools.partial(_dense_kernel, n_i=n_i),
        grid=(t // tm, ne // en),
        in_specs=[pl.BlockSpec((d, tm), lambda i, e: (0, i)),
                  pl.BlockSpec((en, d), lambda i, e: (e, 0)),
                  pl.BlockSpec((d, en), lambda i, e: (0, e)),
                  rt(n_i), rt(n_i), rt(PEER_NKEYS), rt(PEER_NKEYS)],
        out_specs=pl.BlockSpec((d, tm), lambda i, e: (0, i)),
        out_shape=jax.ShapeDtypeStruct((d, t), F32),
        compiler_params=_cparams(("parallel", "arbitrary")),
        name="peer_dense",
    )(h2t_hi, down_bf, up_t_bf, g, l, e1, r1)


def _final_kernel(x1_ref, pt_ref, g2_ref, gf_ref, o_ref):
    x2 = x1_ref[...] + g2_ref[...] * pt_ref[...].T
    ms = jnp.mean(x2 * x2, axis=-1, keepdims=True)
    o_ref[...] = x2 * lax.rsqrt(ms + RMS_EPS) * gf_ref[...]


def _final(x1, peer_t, gate2, g_final):
    bsz, s, d = x1.shape
    tm = min(512, s)
    nt = s // tm
    row = pl.BlockSpec((None, tm, d), lambda b, i: (b, i, 0))
    return pl.pallas_call(
        _final_kernel,
        grid=(bsz, nt),
        in_specs=[row, pl.BlockSpec((d, tm), lambda b, i: (0, b * nt + i)),
                  pl.BlockSpec((None, 1, d), lambda b, i: (b, 0, 0)),
                  pl.BlockSpec((1, d), lambda b, i: (0, 0))],
        out_specs=row,
        out_shape=jax.ShapeDtypeStruct((bsz, s, d), F32),
        compiler_params=_cparams(("parallel", "parallel")),
        name="final",
    )(x1, peer_t, gate2, g_final.reshape(1, d))


def _head_perm(w, axis):
    rep = N_Q_HEADS // N_KV_HEADS
    shape = w.shape
    w = w.reshape(shape[:axis] + (N_KV_HEADS, rep, HEAD_DIM) + shape[axis + 1:])
    return jnp.swapaxes(w, axis, axis + 1).reshape(shape)


def _split_bf16(w):
    hi = w.astype(BF16)
    return hi, (w - hi.astype(F32)).astype(BF16)


def _layer(x, c, positions, w_ada, b_ada, g_mix, w_in, b_in, attn_sinks, w_attn_branch,
           ssm_A_re, ssm_A_im, ssm_log_dt, ssm_B_re, ssm_B_im, ssm_C_re, ssm_C_im, ssm_D,
           w_glu, b_glu, w_ssm_branch, w_out, g_ffn, w_query, sub_keys, expert_down, expert_up):
    bsz, s, d = x.shape
    mod = _ada_mod(c, w_ada, b_ada)
    shift1, scale1, gate1, shift2, scale2, gate2 = [m.reshape(bsz, 1, d) for m in jnp.split(mod, 6, axis=-1)]

    o1, o3, o4 = ATTN_WIDTH, ATTN_WIDTH + 2 * KV_WIDTH, ATTN_WIDTH + 2 * KV_WIDTH + SSM_WIDTH
    permute = lambda w: jnp.concatenate(
        [_head_perm(w[..., :o1], w.ndim - 1), w[..., o3:o4], w[..., o4:], w[..., o1:o3]], axis=-1)
    proj = _in_proj(x, g_mix, scale1, shift1, permute(w_in).astype(BF16), permute(b_in))

    ctab, s1tab, s2tab = _rope_tables(positions)
    attn = _attention(proj, attn_sinks, ctab, s1tab, s2tab)

    ab_re, ab_im, bb_re, bb_im = _ssm_prep(ssm_A_re, ssm_A_im, ssm_log_dt, ssm_B_re, ssm_B_im)
    ns = N_SSM_GROUPS // SSM_SLICE_GROUPS
    eye = jnp.eye(SSM_SLICE_GROUPS, dtype=F32)
    shp = (ns, SSM_SLICE_GROUPS, SSM_GROUP, SSM_STATE)
    blk_b = lambda m: jnp.einsum("sghp,gk->sghkp", m.reshape(shp), eye).reshape(ns, LANES, -1)
    blk_c = lambda m: jnp.einsum("sghp,gk->sgpkh", m.reshape(shp), eye).reshape(ns, -1, LANES)
    wb = jnp.concatenate([blk_b(bb_re), blk_b(bb_im)], axis=-1).astype(BF16)
    wc = jnp.concatenate([blk_c(ssm_C_re), -blk_c(ssm_C_im)], axis=1).astype(BF16)
    u_tm = jnp.swapaxes(proj[:, :, COL_U:COL_U + SSM_WIDTH], 0, 1).reshape(s * bsz, SSM_WIDTH)
    sso_tm = _ssm(u_tm, bsz, wb, wc, ab_re.reshape(1, -1), ab_im.reshape(1, -1), ssm_D,
                  w_glu.astype(BF16), b_glu)
    sso = jnp.swapaxes(sso_tm.reshape(s, bsz, SSM_WIDTH), 0, 1)

    x1, h2t_hi, h2t_lo = _merge(attn, sso, proj, x, gate1, g_ffn, scale2, shift2,
                                _head_perm(w_attn_branch, 0).astype(BF16), w_ssm_branch.astype(BF16),
                                w_out.astype(BF16))

    wq_hi, wq_lo = _split_bf16(w_query.T.reshape(PEER_HEADS, 2 * PEER_HALF, d))
    k_hi, k_lo = _split_bf16(sub_keys)
    g, l, e1, r1 = _peer_route(h2t_hi, h2t_lo, wq_hi, wq_lo, k_hi, k_lo)
    peer_t = _peer_dense(h2t_hi, expert_down.astype(BF16), expert_up.T.astype(BF16), g, l, e1, r1)
    return x1, peer_t, gate2


def kernel(x, c, positions, w_ada, b_ada, g_mix, w_in, b_in, attn_sinks, w_attn_branch, ssm_A_re, ssm_A_im, ssm_log_dt, ssm_B_re, ssm_B_im, ssm_C_re, ssm_C_im, ssm_D, w_glu, b_glu, w_ssm_branch, w_out, g_ffn, w_query, sub_keys, expert_down, expert_up, g_final):
    depth = w_ada.shape[0]
    assert depth == 1
    x1, peer_t, gate2 = _layer(x, c, positions, w_ada[0], b_ada[0], g_mix[0], w_in[0], b_in[0],
                               attn_sinks[0], w_attn_branch[0], ssm_A_re[0], ssm_A_im[0], ssm_log_dt[0],
                               ssm_B_re[0], ssm_B_im[0], ssm_C_re[0], ssm_C_im[0], ssm_D[0], w_glu[0],
                               b_glu[0], w_ssm_branch[0], w_out[0], g_ffn[0], w_query[0], sub_keys[0],
                               expert_down[0], expert_up[0])
    return _final(x1, peer_t, gate2, g_final)
```

```python
import functools
import math

import jax
import jax.numpy as jnp
from jax import lax
from jax.experimental import pallas as pl
from jax.experimental.pallas import tpu as pltpu

F32 = jnp.float32
BF16 = jnp.bfloat16

D_MODEL = 2048
ATTN_WIDTH = 1024
SSM_WIDTH = 1024
HEAD_DIM = 64
N_Q_HEADS = 16
N_KV_HEADS = 4
KV_WIDTH = 256
WINDOW = 128
ROPE_THETA = 500000.0
ROPE_DIMS = 16
SSM_GROUP = 16
N_SSM_GROUPS = 64
SSM_STATE = 64
STATE_LANES = N_SSM_GROUPS * SSM_STATE
PEER_HEADS = 8
PEER_NKEYS = 128
PEER_N = PEER_NKEYS * PEER_NKEYS
PEER_HALF = 128
PEER_TOPK = 16
RMS_EPS = 1e-6
MASK_VALUE = -1e30
IN_COLS = ATTN_WIDTH + 2 * KV_WIDTH + SSM_WIDTH + 2 * D_MODEL

LANES = 128
SSM_SLICE_GROUPS = 8
VMEM_LIMIT = 56 * 1024 * 1024

COL_Q = 0
COL_U = ATTN_WIDTH
COL_GA = COL_U + SSM_WIDTH
COL_GS = COL_GA + D_MODEL
COL_KV = COL_GS + D_MODEL


def _cparams(sem):
    return pltpu.CompilerParams(dimension_semantics=sem, vmem_limit_bytes=VMEM_LIMIT)


def _rms_mod(x, g, scale, shift):
    ms = jnp.mean(x * x, axis=-1, keepdims=True)
    y = x * lax.rsqrt(ms + RMS_EPS) * g
    return y * (1.0 + scale) + shift


def _ada_kernel(c_ref, w_ref, b_ref, o_ref):
    c = c_ref[...]
    sc = c * jax.nn.sigmoid(c)
    o_ref[...] = jnp.dot(sc, w_ref[...], preferred_element_type=F32,
                         precision=lax.Precision.HIGHEST) + b_ref[...]


def _ada_mod(c, w_ada, b_ada):
    bsz, d = c.shape
    n = w_ada.shape[1]
    tn = 1024
    return pl.pallas_call(
        _ada_kernel,
        grid=(n // tn,),
        in_specs=[pl.BlockSpec((bsz, d), lambda j: (0, 0)),
                  pl.BlockSpec((d, tn), lambda j: (0, j)),
                  pl.BlockSpec((1, tn), lambda j: (0, j))],
        out_specs=pl.BlockSpec((bsz, tn), lambda j: (0, j)),
        out_shape=jax.ShapeDtypeStruct((bsz, n), F32),
        compiler_params=_cparams(("parallel",)),
        name="ada_mod",
    )(c, w_ada, b_ada.reshape(1, n))


def _inproj_kernel(x_ref, g_ref, sc_ref, sh_ref, w_ref, b_ref, o_ref, h_ref):
    @pl.when(pl.program_id(2) == 0)
    def _():
        h_ref[...] = _rms_mod(x_ref[...], g_ref[...], sc_ref[...], sh_ref[...]).astype(BF16)

    o_ref[...] = jnp.dot(h_ref[...], w_ref[...], preferred_element_type=F32) + b_ref[...]


def _in_proj(x, g_mix, scale1, shift1, w_in_p, b_in_p):
    bsz, s, d = x.shape
    n = w_in_p.shape[1]
    tm = min(1024, s)
    tn = 512
    return pl.pallas_call(
        _inproj_kernel,
        grid=(bsz, s // tm, n // tn),
        in_specs=[pl.BlockSpec((None, tm, d), lambda b, i, j: (b, i, 0)),
                  pl.BlockSpec((1, d), lambda b, i, j: (0, 0)),
                  pl.BlockSpec((None, 1, d), lambda b, i, j: (b, 0, 0)),
                  pl.BlockSpec((None, 1, d), lambda b, i, j: (b, 0, 0)),
                  pl.BlockSpec((d, tn), lambda b, i, j: (0, j)),
                  pl.BlockSpec((1, tn), lambda b, i, j: (0, j))],
        out_specs=pl.BlockSpec((None, tm, tn), lambda b, i, j: (b, i, j)),
        out_shape=jax.ShapeDtypeStruct((bsz, s, n), F32),
        scratch_shapes=[pltpu.VMEM((tm, d), BF16)],
        compiler_params=_cparams(("parallel", "parallel", "arbitrary")),
        name="in_proj",
    )(x, g_mix.reshape(1, d), scale1, shift1, w_in_p, b_in_p.reshape(1, n))


def _rope_kernel(pos_ref, freq_ref, c_ref, s1_ref, s2_ref):
    ang = pos_ref[...] * freq_ref[...]
    cs = jnp.cos(ang)
    sn = jnp.sin(ang)
    d = lax.broadcasted_iota(jnp.int32, ang.shape, 1) & (HEAD_DIM - 1)
    c_ref[...] = cs
    s1_ref[...] = jnp.where(d < ROPE_DIMS // 2, -sn, 0.0)
    s2_ref[...] = jnp.where((d >= ROPE_DIMS // 2) & (d < ROPE_DIMS), sn, 0.0)


def _rope_tables(positions):
    bsz, s = positions.shape
    t = bsz * s
    tm = min(1024, t)
    inv_freq = ROPE_THETA ** (-(jnp.arange(0, ROPE_DIMS, 2, dtype=F32) / ROPE_DIMS))
    d = jnp.arange(LANES) % HEAD_DIM
    freq = jnp.where(d < ROPE_DIMS, inv_freq[d % (ROPE_DIMS // 2)], 0.0).astype(F32).reshape(1, LANES)
    pos = positions.astype(F32).reshape(t, 1)
    sds = jax.ShapeDtypeStruct((t, LANES), F32)
    outs = pl.pallas_call(
        _rope_kernel,
        grid=(t // tm,),
        in_specs=[pl.BlockSpec((tm, 1), lambda i: (i, 0)),
                  pl.BlockSpec((1, LANES), lambda i: (0, 0))],
        out_specs=[pl.BlockSpec((tm, LANES), lambda i: (i, 0))] * 3,
        out_shape=[sds, sds, sds],
        compiler_params=_cparams(("parallel",)),
        name="rope_tables",
    )(pos, freq)
    return [o.reshape(bsz, s, LANES) for o in outs]


def _rope_apply(t, c, s1, s2):
    outs = []
    for j in range(t.shape[1] // LANES):
        tj = t[:, j * LANES:(j + 1) * LANES]
        outs.append(tj * c + pltpu.roll(tj, LANES - ROPE_DIMS // 2, 1) * s1
                    + pltpu.roll(tj, ROPE_DIMS // 2, 1) * s2)
    return jnp.concatenate(outs, axis=1)


def _attn_kernel(sink_ref, q_ref, kvc_ref, kvp_ref, cq_ref, s1q_ref, s2q_ref,
                 cp_ref, s1p_ref, s2p_ref, bm_ref, o_ref):
    n = pl.program_id(1)
    cq, s1q, s2q = cq_ref[...], s1q_ref[...], s2q_ref[...]
    q = (_rope_apply(q_ref[...], cq, s1q, s2q) * (HEAD_DIM ** -0.5)).astype(BF16)
    kvc = kvc_ref[...]
    kvp = kvp_ref[...]
    kc = _rope_apply(kvc[:, :KV_WIDTH], cq, s1q, s2q)
    kp = _rope_apply(kvp[:, :KV_WIDTH], cp_ref[...], s1p_ref[...], s2p_ref[...])
    kband = jnp.concatenate([kp, kc], axis=0).astype(BF16)
    vband = jnp.concatenate([kvp[:, KV_WIDTH:], kvc[:, KV_WIDTH:]], axis=0).astype(BF16)
    bm = bm_ref[...]
    kexp = jnp.concatenate([kband] * N_KV_HEADS, axis=0) * bm
    vexp = jnp.concatenate([vband] * N_KV_HEADS, axis=0) * bm
    row = lax.broadcasted_iota(jnp.int32, (WINDOW, 2 * WINDOW), 0)
    col = lax.broadcasted_iota(jnp.int32, (WINDOW, 2 * WINDOW), 1)
    diff = row + WINDOW - col
    valid = (diff >= 0) & (diff < WINDOW) & ((col >= WINDOW) | (n > 0))
    rep = N_Q_HEADS // N_KV_HEADS
    kw = 2 * WINDOW
    for r in range(rep):
        qr = q[:, r * KV_WIDTH:(r + 1) * KV_WIDTH]
        sc = lax.dot_general(qr, kexp, (((1,), (1,)), ((), ())), preferred_element_type=F32)
        ps = []
        for g in range(N_KV_HEADS):
            s = jnp.where(valid, sc[:, g * kw:(g + 1) * kw], MASK_VALUE)
            sink = sink_ref[g * rep + r]
            m = jnp.maximum(jnp.max(s, axis=-1, keepdims=True), sink)
            p = jnp.exp(s - m)
            l = jnp.sum(p, axis=-1, keepdims=True) + jnp.exp(sink - m)
            ps.append((p / l).astype(BF16))
        pcat = jnp.concatenate(ps, axis=1)
        o = jnp.dot(pcat, vexp, preferred_element_type=F32)
        o_ref[:, r * KV_WIDTH:(r + 1) * KV_WIDTH] = o.astype(BF16)


def _attention(proj, sinks, ctab, s1tab, s2tab):
    bsz, s, _ = proj.shape
    nb = s // WINDOW
    g_row = jnp.arange(N_KV_HEADS * 2 * WINDOW) // (2 * WINDOW)
    g_col = jnp.arange(KV_WIDTH) // HEAD_DIM
    bm = (g_row[:, None] == g_col[None, :]).astype(BF16)
    cur = lambda b, n: (b, n, 0)
    prev = lambda b, n: (b, jnp.maximum(n - 1, 0), 0)
    kvw = 2 * KV_WIDTH
    tab = pl.BlockSpec((None, WINDOW, LANES), cur)
    tabp = pl.BlockSpec((None, WINDOW, LANES), prev)
    return pl.pallas_call(
        _attn_kernel,
        grid=(bsz, nb),
        in_specs=[pl.BlockSpec(memory_space=pltpu.SMEM),
                  pl.BlockSpec((None, WINDOW, ATTN_WIDTH), cur),
                  pl.BlockSpec((None, WINDOW, kvw), lambda b, n: (b, n, COL_KV // kvw)),
                  pl.BlockSpec((None, WINDOW, kvw), lambda b, n: (b, jnp.maximum(n - 1, 0), COL_KV // kvw)),
                  tab, tab, tab, tabp, tabp, tabp,
                  pl.BlockSpec((N_KV_HEADS * 2 * WINDOW, KV_WIDTH), lambda b, n: (0, 0))],
        out_specs=pl.BlockSpec((None, WINDOW, ATTN_WIDTH), cur),
        out_shape=jax.ShapeDtypeStruct((bsz, s, ATTN_WIDTH), BF16),
        compiler_params=_cparams(("parallel", "parallel")),
        name="attention",
    )(sinks, proj, proj, proj, ctab, s1tab, s2tab, ctab, s1tab, s2tab, bm)


def _ssm_prep_kernel(are_ref, aim_ref, ldt_ref, bre_ref, bim_ref,
                     abre_ref, abim_ref, bbre_ref, bbim_ref):
    dt = jnp.exp(ldt_ref[...])
    lam_re = jnp.minimum(are_ref[...], -1e-4)
    lam_im = aim_ref[...]
    mag = jnp.exp(lam_re * dt)
    ab_re = mag * jnp.cos(lam_im * dt)
    ab_im = mag * jnp.sin(lam_im * dt)
    den = lam_re * lam_re + lam_im * lam_im
    num_re = ab_re - 1.0
    z_re = (num_re * lam_re + ab_im * lam_im) / den
    z_im = (ab_im * lam_re - num_re * lam_im) / den
    abre_ref[...] = ab_re
    abim_ref[...] = ab_im
    bre = bre_ref[...]
    bim = bim_ref[...]
    bbre_ref[...] = z_re * bre - z_im * bim
    bbim_ref[...] = z_re * bim + z_im * bre


def _ssm_prep(a_re, a_im, log_dt, b_re, b_im):
    g, p, h = b_re.shape
    a3 = jax.ShapeDtypeStruct((g, 1, p), F32)
    b3 = jax.ShapeDtypeStruct((g, h, p), F32)
    return pl.pallas_call(
        _ssm_prep_kernel,
        out_shape=[a3, a3, b3, b3],
        name="ssm_prep",
    )(a_re.reshape(g, 1, p), a_im.reshape(g, 1, p), log_dt.reshape(g, 1, 1),
      jnp.swapaxes(b_re, 1, 2), jnp.swapaxes(b_im, 1, 2))


def _ssm_kernel(u_ref, wb_ref, wc_ref, are_ref, aim_ref, d_ref, wg_ref, bg_ref, o_ref,
                xre, xim, sre, sim, *, n_batch, t_chunk, lane_chunk):
    @pl.when(pl.program_id(0) == 0)
    def _():
        sre[...] = jnp.zeros_like(sre)
        sim[...] = jnp.zeros_like(sim)

    n_slices = SSM_WIDTH // LANES
    sw = SSM_SLICE_GROUPS * SSM_STATE
    u = u_ref[...]
    ub = u.astype(BF16)
    for s in range(n_slices):
        bu = jnp.dot(ub[:, s * LANES:(s + 1) * LANES], wb_ref[s], preferred_element_type=F32)
        xre[:, s * sw:(s + 1) * sw] = bu[:, :sw]
        xim[:, s * sw:(s + 1) * sw] = bu[:, sw:]

    for lc in range(STATE_LANES // lane_chunk):
        sl = slice(lc * lane_chunk, (lc + 1) * lane_chunk)
        ar = jnp.broadcast_to(are_ref[:, sl], (n_batch, lane_chunk))
        ai = jnp.broadcast_to(aim_ref[:, sl], (n_batch, lane_chunk))

        def body(t, carry, sl=sl, ar=ar, ai=ai):
            cr, ci = carry
            off = pl.multiple_of(t * n_batch, n_batch)
            nr = ar * cr - ai * ci + xre[pl.ds(off, n_batch), sl]
            ni = ar * ci + ai * cr + xim[pl.ds(off, n_batch), sl]
            xre[pl.ds(off, n_batch), sl] = nr
            xim[pl.ds(off, n_batch), sl] = ni
            return nr, ni

        cr, ci = lax.fori_loop(0, t_chunk, body, (sre[:, sl], sim[:, sl]), unroll=8)
        sre[:, sl] = cr
        sim[:, sl] = ci

    ys = []
    for s in range(n_slices):
        xs = jnp.concatenate([xre[:, s * sw:(s + 1) * sw], xim[:, s * sw:(s + 1) * sw]], axis=1)
        ys.append(jnp.dot(xs.astype(BF16), wc_ref[s], preferred_element_type=F32))
    y = jnp.concatenate(ys, axis=1) + d_ref[...] * u
    z = jax.nn.gelu(y)
    gl = jnp.dot(z.astype(BF16), wg_ref[...], preferred_element_type=F32) + bg_ref[...]
    o_ref[...] = (z * jax.nn.sigmoid(gl)).astype(BF16)


def _ssm(u_tm, n_batch, wb, wc, ab_re, ab_im, d_skip, w_glu, b_glu):
    rows, w = u_tm.shape
    s = rows // n_batch
    t_chunk = min(64, s)
    r = t_chunk * n_batch
    full = lambda shape: pl.BlockSpec(shape, lambda i: (0,) * len(shape))
    kern = functools.partial(_ssm_kernel, n_batch=n_batch, t_chunk=t_chunk, lane_chunk=512)
    return pl.pallas_call(
        kern,
        grid=(s // t_chunk,),
        in_specs=[pl.BlockSpec((r, w), lambda i: (i, 0)),
                  full(wb.shape), full(wc.shape), full((1, STATE_LANES)), full((1, STATE_LANES)),
                  full((1, w)), full((w, w)), full((1, w))],
        out_specs=pl.BlockSpec((r, w), lambda i: (i, 0)),
        out_shape=jax.ShapeDtypeStruct((rows, w), BF16),
        scratch_shapes=[pltpu.VMEM((r, STATE_LANES), F32), pltpu.VMEM((r, STATE_LANES), F32),
                        pltpu.VMEM((n_batch, STATE_LANES), F32), pltpu.VMEM((n_batch, STATE_LANES), F32)],
        compiler_params=_cparams(("arbitrary",)),
        name="ssm",
    )(u_tm, wb, wc, ab_re, ab_im, d_skip.reshape(1, w), w_glu, b_glu.reshape(1, w))


def _merge_kernel(attn_ref, sso_ref, ga_ref, gs_ref, x_ref, g1_ref, gf_ref, sc2_ref, sh2_ref,
                  wab_ref, wsb_ref, wo_ref, x1_ref, hi_ref, lo_ref):
    ya = jnp.dot(attn_ref[...], wab_ref[...], preferred_element_type=F32)
    ys = jnp.dot(sso_ref[...], wsb_ref[...], preferred_element_type=F32)
    merged = jax.nn.sigmoid(ga_ref[...]) * ya + jax.nn.sigmoid(gs_ref[...]) * ys
    mo = jnp.dot(merged.astype(BF16), wo_ref[...], preferred_element_type=F32)
    x1 = x_ref[...] + g1_ref[...] * mo
    x1_ref[...] = x1
    h2t = _rms_mod(x1, gf_ref[...], sc2_ref[...], sh2_ref[...]).T
    hi = h2t.astype(BF16)
    hi_ref[...] = hi
    lo_ref[...] = (h2t - hi.astype(F32)).astype(BF16)


def _merge(attn, sso, proj, x, gate1, g_ffn, scale2, shift2, wab, wsb, wo):
    bsz, s, d = x.shape
    tm = min(256, s)
    nt = s // tm
    row = lambda w, cb: pl.BlockSpec((None, tm, w), lambda b, i: (b, i, cb))
    per_b = pl.BlockSpec((None, 1, d), lambda b, i: (b, 0, 0))
    const = lambda shape: pl.BlockSpec(shape, lambda b, i: (0, 0), pipeline_mode=pl.Buffered(1))
    tsds = jax.ShapeDtypeStruct((d, bsz * s), BF16)
    tspec = pl.BlockSpec((d, tm), lambda b, i: (0, b * nt + i))
    return pl.pallas_call(
        _merge_kernel,
        grid=(bsz, nt),
        in_specs=[row(ATTN_WIDTH, 0), row(SSM_WIDTH, 0), row(d, COL_GA // d), row(d, COL_GS // d),
                  row(d, 0), per_b, pl.BlockSpec((1, d), lambda b, i: (0, 0)), per_b, per_b,
                  const(wab.shape), const(wsb.shape), const(wo.shape)],
        out_specs=[row(d, 0), tspec, tspec],
        out_shape=[jax.ShapeDtypeStruct((bsz, s, d), F32), tsds, tsds],
        compiler_params=_cparams(("parallel", "parallel")),
        name="merge",
    )(attn, sso, proj, proj, x, gate1, g_ffn.reshape(1, d), scale2, shift2, wab, wsb, wo)


def _dot3(a_hi, a_lo, b_hi, b_lo):
    return (jnp.dot(a_hi, b_hi, preferred_element_type=F32)
            + jnp.dot(a_hi, b_lo, preferred_element_type=F32)
            + jnp.dot(a_lo, b_hi, preferred_element_type=F32))


def _top16(v):
    idx = lax.broadcasted_iota(jnp.int32, v.shape, 0)
    rank = jnp.full(v.shape, 100.0, F32)
    vals = []
    for it in range(PEER_TOPK):
        m = jnp.max(v, axis=0, keepdims=True)
        istar = jnp.min(jnp.where(v == m, idx, PEER_NKEYS), axis=0, keepdims=True)
        sel = idx == istar
        rank = jnp.where(sel, float(it + 1), rank)
        v = jnp.where(sel, -jnp.inf, v)
        vals.append(m)
    return jnp.concatenate(vals, axis=0), rank


def _staircase(a, b):
    lanes = a.shape[1]
    rowi = lax.broadcasted_iota(jnp.int32, (8, lanes), 0)
    cs, codes = [], []
    for q in range(PEER_TOPK):
        nq = PEER_TOPK // (q + 1)
        for blk in range((nq + 7) // 8):
            p = rowi + 8 * blk
            c = a[8 * blk:8 * blk + 8, :] + b[q:q + 1, :]
            cs.append(jnp.where(p < nq, c, -jnp.inf))
            codes.append(p * PEER_TOPK + q)
    cand = jnp.concatenate(cs, axis=0)
    code = jnp.concatenate(codes, axis=0)
    row16 = lax.broadcasted_iota(jnp.int32, (PEER_TOPK, lanes), 0)
    cnt = jnp.zeros((PEER_TOPK, lanes), F32)
    m0 = a[0:1, :] + b[0:1, :]
    z = jnp.zeros((1, lanes), F32)
    for _ in range(PEER_TOPK):
        m = jnp.max(cand, axis=0, keepdims=True)
        cstar = jnp.min(jnp.where(cand == m, code, 1 << 20), axis=0, keepdims=True)
        cand = jnp.where(code == cstar, -jnp.inf, cand)
        z = z + jnp.exp(m - m0)
        cnt = cnt + jnp.where(row16 == (cstar >> 4), 1.0, 0.0)
    return cnt, z


def _batcher_pairs(n):
    pairs = []

    def merge(lo, hi, r):
        step = r * 2
        if step < hi - lo:
            merge(lo, hi, step)
            merge(lo + r, hi, step)
            pairs.extend((i, i + r) for i in range(lo + r, hi - r, step))
        else:
            pairs.append((lo, lo + r))

    def sort(lo, hi):
        if hi - lo >= 1:
            mid = lo + (hi - lo) // 2
            sort(lo, mid)
            sort(mid + 1, hi)
            merge(lo, hi, 1)

    sort(0, n - 1)
    return pairs


_SORT16 = _batcher_pairs(PEER_TOPK)
SUBLANES = 8


def _cmpx(v, i, j):
    hi, lo = jnp.maximum(v[i], v[j]), jnp.minimum(v[i], v[j])
    v[i], v[j] = hi, lo


def _sub_allreduce(x, op):
    for shift in (4, 2, 1):
        x = op(x, pltpu.roll(x, shift, 0))
    return x


def _sorted_top16(s):
    v = [s[SUBLANES * k:SUBLANES * (k + 1), :] for k in range(PEER_TOPK)]
    for i, j in _SORT16:
        _cmpx(v, i, j)
    for shift in (4, 2, 1):
        m = [jnp.maximum(v[k], pltpu.roll(v[PEER_TOPK - 1 - k], shift, 0)) for k in range(PEER_TOPK)]
        for d in (8, 4, 2, 1):
            for i in range(PEER_TOPK):
                if not i & d:
                    _cmpx(m, i, i + d)
        v = m
    return v


def _count_sorted(b, passes):
    t3 = passes(b[7])
    t2 = passes(jnp.where(t3, b[11], b[3]))
    t1 = passes(jnp.where(t3, jnp.where(t2, b[13], b[9]), jnp.where(t2, b[5], b[1])))
    c0 = jnp.where(t3,
                   jnp.where(t2, jnp.where(t1, b[14], b[12]), jnp.where(t1, b[10], b[8])),
                   jnp.where(t2, jnp.where(t1, b[6], b[4]), jnp.where(t1, b[2], b[0])))
    t0 = passes(c0)
    cnt = (jnp.where(t3, 8.0, 0.0) + jnp.where(t2, 4.0, 0.0)
           + jnp.where(t1, 2.0, 0.0) + jnp.where(t0, 1.0, 0.0))
    return jnp.where(passes(b[15]), 16.0, cnt)


def _route_fast(s0, s1):
    a = _sorted_top16(s0)
    b = _sorted_top16(s1)
    sub = lax.broadcasted_iota(jnp.int32, a[0].shape, 0)
    ninf = -jnp.inf

    def pick(vs, base):
        acc = vs[base + SUBLANES - 1]
        for r in range(SUBLANES - 2, -1, -1):
            acc = jnp.where(sub == r, vs[base + r], acc)
        return acc

    a_lo, a_hi, b_lo, b_hi = pick(a, 0), pick(a, 8), pick(b, 0), pick(b, 8)
    ge2 = sub >= 2
    packs = [a[0] + b_lo, a[0] + b_hi, a[1] + b_lo, a_hi + b[0],
             jnp.where(ge2, a_lo + b[0], ninf), jnp.where(ge2, a_lo + b[1], ninf),
             jnp.where(ge2 & (sub <= 4), a_lo + b[2], ninf),
             jnp.where((sub == 3) | (sub == 4), a[2] + b_lo, jnp.where(sub == 5, a[3] + b[3], ninf))]
    work = list(packs)
    tau = None
    for it in range(PEER_TOPK):
        m = work[0]
        for w in work[1:]:
            m = jnp.maximum(m, w)
        tau = _sub_allreduce(m, jnp.maximum)
        if it < PEER_TOPK - 1:
            work = [jnp.where(w == tau, ninf, w) for w in work]
    m0 = a[0] + b[0]
    z = None
    for p in packs:
        term = jnp.where(p >= tau, jnp.exp(p - m0), 0.0)
        z = term if z is None else z + term
    z = _sub_allreduce(z, jnp.add)

    g, l, e1, r1 = [], [], [], []
    lsum = c16 = rsum = None
    for k in range(PEER_NKEYS // SUBLANES):
        s0v = s0[SUBLANES * k:SUBLANES * (k + 1), :]
        s1v = s1[SUBLANES * k:SUBLANES * (k + 1), :]
        lv = _count_sorted(b, lambda x, s0v=s0v: (s0v + x) >= tau)
        rv = _count_sorted(b, lambda x, s1v=s1v: x > s1v) + 1.0
        in16 = rv <= float(PEER_TOPK)
        one = jnp.where(in16, 1.0, 0.0)
        rin = jnp.where(in16, rv, 0.0)
        lsum = lv if lsum is None else lsum + lv
        c16 = one if c16 is None else c16 + one
        rsum = rin if rsum is None else rsum + rin
        g.append(jnp.exp(s0v - a[0]) / z)
        l.append(lv)
        e1.append(jnp.exp(s1v - b[0]))
        r1.append(rv)
    lsum = _sub_allreduce(lsum, jnp.add)
    c16 = _sub_allreduce(c16, jnp.add)
    rsum = _sub_allreduce(rsum, jnp.add)
    n = float(PEER_TOPK)
    clean = (lsum == n) & (c16 == n) & (rsum == n * (n + 1.0) / 2.0)
    return g, l, e1, r1, jnp.where(clean, 0.0, 1.0)


def _route_exact(s0, s1):
    a, rank0 = _top16(s0)
    b, rank1 = _top16(s1)
    cnt, z = _staircase(a, b)
    lsel = jnp.zeros(s0.shape, F32)
    for p in range(PEER_TOPK):
        lsel = lsel + jnp.where(rank0 == float(p + 1), cnt[p:p + 1, :], 0.0)
    return jnp.exp(s0 - a[0:1, :]) / z, lsel, jnp.exp(s1 - b[0:1, :]), rank1


def _kw_kernel(kh_ref, kl_ref, wh_ref, wl_ref, o_ref):
    o_ref[...] = _dot3(kh_ref[...], kl_ref[...], wh_ref[...], wl_ref[...])


def _key_query_weights(k_hi, k_lo, wq_hi, wq_lo):
    n, nk, hd = k_hi.shape
    d = wq_hi.shape[-1]
    ks = pl.BlockSpec((None, nk, hd), lambda i: (i, 0, 0))
    ws = pl.BlockSpec((None, hd, d), lambda i: (i, 0, 0))
    return pl.pallas_call(
        _kw_kernel,
        grid=(n,),
        in_specs=[ks, ks, ws, ws],
        out_specs=pl.BlockSpec((None, nk, d), lambda i: (i, 0, 0)),
        out_shape=jax.ShapeDtypeStruct((n, nk, d), F32),
        compiler_params=_cparams(("parallel",)),
        name="peer_kw",
    )(k_hi, k_lo, wq_hi, wq_lo)


def _route_kernel(xh_ref, xl_ref, wh_ref, wl_ref, g_ref, l_ref, e1_ref, r1_ref, sc_ref):
    sc_ref[...] = _dot3(wh_ref[...], wl_ref[...], xh_ref[...], xl_ref[...])
    ng = sc_ref.shape[1] // LANES

    def lane_group(idx, carry):
        h = idx // ng
        off = pl.multiple_of((idx % ng) * LANES, LANES)
        row = pl.multiple_of(h * (2 * PEER_NKEYS), 2 * PEER_NKEYS)
        s0 = sc_ref[pl.ds(row, PEER_NKEYS), pl.ds(off, LANES)]
        s1 = sc_ref[pl.ds(row + PEER_NKEYS, PEER_NKEYS), pl.ds(off, LANES)]
        g, l, e1, r1, bad = _route_fast(s0, s1)
        cat = lambda pieces: jnp.concatenate(pieces, axis=0)
        g_ref[h, :, pl.ds(off, LANES)] = cat(g)
        l_ref[h, :, pl.ds(off, LANES)] = cat(l)
        e1_ref[h, :, pl.ds(off, LANES)] = cat(e1).astype(BF16)
        r1_ref[h, :, pl.ds(off, LANES)] = cat(r1).astype(BF16)

        @pl.when(jnp.max(bad) > 0.0)
        def _():
            gx, lx, e1x, r1x = _route_exact(s0, s1)
            g_ref[h, :, pl.ds(off, LANES)] = gx
            l_ref[h, :, pl.ds(off, LANES)] = lx
            e1_ref[h, :, pl.ds(off, LANES)] = e1x.astype(BF16)
            r1_ref[h, :, pl.ds(off, LANES)] = r1x.astype(BF16)

        return carry

    lax.fori_loop(0, PEER_HEADS * ng, lane_group, 0)


def _peer_route(h2t_hi, h2t_lo, kw_hi, kw_lo):
    d, t = h2t_hi.shape
    tn = min(256, t)
    nq = kw_hi.shape[0]
    xs = pl.BlockSpec((d, tn), lambda i: (0, i))
    ws = pl.BlockSpec((nq, d), lambda i: (0, 0), pipeline_mode=pl.Buffered(1))
    os_ = pl.BlockSpec((PEER_HEADS, PEER_NKEYS, tn), lambda i: (0, 0, i))
    f = jax.ShapeDtypeStruct((PEER_HEADS, PEER_NKEYS, t), F32)
    bf = jax.ShapeDtypeStruct((PEER_HEADS, PEER_NKEYS, t), BF16)
    return pl.pallas_call(
        _route_kernel,
        grid=(t // tn,),
        in_specs=[xs, xs, ws, ws],
        out_specs=[os_, os_, os_, os_],
        out_shape=[f, f, bf, bf],
        scratch_shapes=[pltpu.VMEM((nq, tn), F32)],
        compiler_params=_cparams(("parallel",)),
        name="peer_route",
    )(h2t_hi, h2t_lo, kw_hi, kw_lo)


def _dense_kernel(h_ref, dn_ref, up_ref, g_ref, l_ref, e1_ref, r1_ref, o_ref, *, n_i):
    @pl.when(pl.program_id(1) == 0)
    def _():
        o_ref[...] = jnp.zeros_like(o_ref)

    hx = h_ref[...]
    wgs = []
    ipc = 1
    for ic in range(n_i // ipc):
        rows = slice(ic * ipc * PEER_NKEYS, (ic + 1) * ipc * PEER_NKEYS)
        at = jnp.dot(dn_ref[rows, :], hx, preferred_element_type=F32)
        ws = []
        for il in range(ic * ipc, (ic + 1) * ipc):
            w = None
            for h in range(PEER_HEADS):
                gi = g_ref[h, il:il + 1, :].astype(BF16)
                li = l_ref[h, il:il + 1, :].astype(BF16)
                term = jnp.where(r1_ref[h] <= li, e1_ref[h] * gi, jnp.zeros((), BF16))
                w = term if w is None else w + term
            ws.append(w)
        wgs.append(jax.nn.gelu(at).astype(BF16) * jnp.concatenate(ws, axis=0))
    wg = jnp.concatenate(wgs, axis=0)
    o_ref[...] += jnp.dot(up_ref[...], wg, preferred_element_type=F32)


def _peer_dense(h2t_hi, down_bf, up_t_bf, g, l, e1, r1):
    d, t = h2t_hi.shape
    ne = down_bf.shape[0]
    tm = min(512, t)
    en = 1024
    n_i = en // PEER_NKEYS
    rt = lambda n: pl.BlockSpec((PEER_HEADS, n, tm), lambda i, e: (0, e if n == n_i else 0, i))
    return pl.pallas_call(
        functools.partial(_dense_kernel, n_i=n_i),
        grid=(t // tm, ne // en),
        in_specs=[pl.BlockSpec((d, tm), lambda i, e: (0, i)),
                  pl.BlockSpec((en, d), lambda i, e: (e, 0)),
                  pl.BlockSpec((d, en), lambda i, e: (0, e)),
                  rt(n_i), rt(n_i), rt(PEER_NKEYS), rt(PEER_NKEYS)],
        out_specs=pl.BlockSpec((d, tm), lambda i, e: (0, i)),
        out_shape=jax.ShapeDtypeStruct((d, t), F32),
        compiler_params=_cparams(("parallel", "arbitrary")),
        name="peer_dense",
    )(h2t_hi, down_bf, up_t_bf, g, l, e1, r1)


def _final_kernel(x1_ref, pt_ref, g2_ref, gf_ref, o_ref):
    x2 = x1_ref[...] + g2_ref[...] * pt_ref[...].T
    ms = jnp.mean(x2 * x2, axis=-1, keepdims=True)
    o_ref[...] = x2 * lax.rsqrt(ms + RMS_EPS) * gf_ref[...]


def _final(x1, peer_t, gate2, g_final):
    bsz, s, d = x1.shape
    tm = min(512, s)
    nt = s // tm
    row = pl.BlockSpec((None, tm, d), lambda b, i: (b, i, 0))
    return pl.pallas_call(
        _final_kernel,
        grid=(bsz, nt),
        in_specs=[row, pl.BlockSpec((d, tm), lambda b, i: (0, b * nt + i)),
                  pl.BlockSpec((None, 1, d), lambda b, i: (b, 0, 0)),
                  pl.BlockSpec((1, d), lambda b, i: (0, 0))],
        out_specs=row,
        out_shape=jax.ShapeDtypeStruct((bsz, s, d), F32),
        compiler_params=_cparams(("parallel", "parallel")),
        name="final",
    )(x1, peer_t, gate2, g_final.reshape(1, d))


def _head_perm(w, axis):
    rep = N_Q_HEADS // N_KV_HEADS
    shape = w.shape
    w = w.reshape(shape[:axis] + (N_KV_HEADS, rep, HEAD_DIM) + shape[axis + 1:])
    return jnp.swapaxes(w, axis, axis + 1).reshape(shape)


def _split_bf16(w):
    hi = w.astype(BF16)
    return hi, (w - hi.astype(F32)).astype(BF16)


def _layer(x, c, positions, w_ada, b_ada, g_mix, w_in, b_in, attn_sinks, w_attn_branch,
           ssm_A_re, ssm_A_im, ssm_log_dt, ssm_B_re, ssm_B_im, ssm_C_re, ssm_C_im, ssm_D,
           w_glu, b_glu, w_ssm_branch, w_out, g_ffn, w_query, sub_keys, expert_down, expert_up):
    bsz, s, d = x.shape
    mod = _ada_mod(c, w_ada, b_ada)
    shift1, scale1, gate1, shift2, scale2, gate2 = [m.reshape(bsz, 1, d) for m in jnp.split(mod, 6, axis=-1)]

    o1, o3, o4 = ATTN_WIDTH, ATTN_WIDTH + 2 * KV_WIDTH, ATTN_WIDTH + 2 * KV_WIDTH + SSM_WIDTH
    permute = lambda w: jnp.concatenate(
        [_head_perm(w[..., :o1], w.ndim - 1), w[..., o3:o4], w[..., o4:], w[..., o1:o3]], axis=-1)
    proj = _in_proj(x, g_mix, scale1, shift1, permute(w_in).astype(BF16), permute(b_in))

    ctab, s1tab, s2tab = _rope_tables(positions)
    attn = _attention(proj, attn_sinks, ctab, s1tab, s2tab)

    ab_re, ab_im, bb_re, bb_im = _ssm_prep(ssm_A_re, ssm_A_im, ssm_log_dt, ssm_B_re, ssm_B_im)
    ns = N_SSM_GROUPS // SSM_SLICE_GROUPS
    eye = jnp.eye(SSM_SLICE_GROUPS, dtype=F32)
    shp = (ns, SSM_SLICE_GROUPS, SSM_GROUP, SSM_STATE)
    blk_b = lambda m: jnp.einsum("sghp,gk->sghkp", m.reshape(shp), eye).reshape(ns, LANES, -1)
    blk_c = lambda m: jnp.einsum("sghp,gk->sgpkh", m.reshape(shp), eye).reshape(ns, -1, LANES)
    wb = jnp.concatenate([blk_b(bb_re), blk_b(bb_im)], axis=-1).astype(BF16)
    wc = jnp.concatenate([blk_c(ssm_C_re), -blk_c(ssm_C_im)], axis=1).astype(BF16)
    u_tm = jnp.swapaxes(proj[:, :, COL_U:COL_U + SSM_WIDTH], 0, 1).reshape(s * bsz, SSM_WIDTH)
    sso_tm = _ssm(u_tm, bsz, wb, wc, ab_re.reshape(1, -1), ab_im.reshape(1, -1), ssm_D,
                  w_glu.astype(BF16), b_glu)
    sso = jnp.swapaxes(sso_tm.reshape(s, bsz, SSM_WIDTH), 0, 1)

    x1, h2t_hi, h2t_lo = _merge(attn, sso, proj, x, gate1, g_ffn, scale2, shift2,
                                _head_perm(w_attn_branch, 0).astype(BF16), w_ssm_branch.astype(BF16),
                                w_out.astype(BF16))

    wq_hi, wq_lo = _split_bf16(w_query.T.reshape(2 * PEER_HEADS, PEER_HALF, d))
    k_hi, k_lo = _split_bf16(sub_keys.reshape(2 * PEER_HEADS, PEER_NKEYS, PEER_HALF))
    kw_hi, kw_lo = _split_bf16(_key_query_weights(k_hi, k_lo, wq_hi, wq_lo).reshape(-1, d))
    g, l, e1, r1 = _peer_route(h2t_hi, h2t_lo, kw_hi, kw_lo)
    peer_t = _peer_dense(h2t_hi, expert_down.astype(BF16), expert_up.T.astype(BF16), g, l, e1, r1)
    return x1, peer_t, gate2


def kernel(x, c, positions, w_ada, b_ada, g_mix, w_in, b_in, attn_sinks, w_attn_branch, ssm_A_re, ssm_A_im, ssm_log_dt, ssm_B_re, ssm_B_im, ssm_C_re, ssm_C_im, ssm_D, w_glu, b_glu, w_ssm_branch, w_out, g_ffn, w_query, sub_keys, expert_down, expert_up, g_final):
    depth = w_ada.shape[0]
    assert depth == 1
    x1, peer_t, gate2 = _layer(x, c, positions, w_ada[0], b_ada[0], g_mix[0], w_in[0], b_in[0],
                               attn_sinks[0], w_attn_branch[0], ssm_A_re[0], ssm_A_im[0], ssm_log_dt[0],
                               ssm_B_re[0], ssm_B_im[0], ssm_C_re[0], ssm_C_im[0], ssm_D[0], w_glu[0],
                               b_glu[0], w_ssm_branch[0], w_out[0], g_ffn[0], w_query[0], sub_keys[0],
                               expert_down[0], expert_up[0])
    return _final(x1, peer_t, gate2, g_final)
```

```python
import functools
import math

import jax
import jax.numpy as jnp
from jax import lax
from jax.experimental import pallas as pl
from jax.experimental.pallas import tpu as pltpu

F32 = jnp.float32
BF16 = jnp.bfloat16

D_MODEL = 2048
ATTN_WIDTH = 1024
SSM_WIDTH = 1024
HEAD_DIM = 64
N_Q_HEADS = 16
N_KV_HEADS = 4
KV_WIDTH = 256
WINDOW = 128
ROPE_THETA = 500000.0
ROPE_DIMS = 16
SSM_GROUP = 16
N_SSM_GROUPS = 64
SSM_STATE = 64
STATE_LANES = N_SSM_GROUPS * SSM_STATE
PEER_HEADS = 8
PEER_NKEYS = 128
PEER_N = PEER_NKEYS * PEER_NKEYS
PEER_HALF = 128
PEER_TOPK = 16
RMS_EPS = 1e-6
MASK_VALUE = -1e30
IN_COLS = ATTN_WIDTH + 2 * KV_WIDTH + SSM_WIDTH + 2 * D_MODEL

LANES = 128
SSM_SLICE_GROUPS = 8
VMEM_LIMIT = 56 * 1024 * 1024

COL_Q = 0
COL_U = ATTN_WIDTH
COL_GA = COL_U + SSM_WIDTH
COL_GS = COL_GA + D_MODEL
COL_KV = COL_GS + D_MODEL


def _cparams(sem):
    return pltpu.CompilerParams(dimension_semantics=sem, vmem_limit_bytes=VMEM_LIMIT)


def _rms_mod(x, g, scale, shift):
    ms = jnp.mean(x * x, axis=-1, keepdims=True)
    y = x * lax.rsqrt(ms + RMS_EPS) * g
    return y * (1.0 + scale) + shift


def _ada_kernel(c_ref, w_ref, b_ref, o_ref):
    c = c_ref[...]
    sc = c * jax.nn.sigmoid(c)
    o_ref[...] = jnp.dot(sc, w_ref[...], preferred_element_type=F32,
                         precision=lax.Precision.HIGHEST) + b_ref[...]


def _ada_mod(c, w_ada, b_ada):
    bsz, d = c.shape
    n = w_ada.shape[1]
    tn = 1024
    return pl.pallas_call(
        _ada_kernel,
        grid=(n // tn,),
        in_specs=[pl.BlockSpec((bsz, d), lambda j: (0, 0)),
                  pl.BlockSpec((d, tn), lambda j: (0, j)),
                  pl.BlockSpec((1, tn), lambda j: (0, j))],
        out_specs=pl.BlockSpec((bsz, tn), lambda j: (0, j)),
        out_shape=jax.ShapeDtypeStruct((bsz, n), F32),
        compiler_params=_cparams(("parallel",)),
        name="ada_mod",
    )(c, w_ada, b_ada.reshape(1, n))


def _inproj_kernel(x_ref, g_ref, sc_ref, sh_ref, w_ref, b_ref, o_ref, h_ref):
    @pl.when(pl.program_id(2) == 0)
    def _():
        h_ref[...] = _rms_mod(x_ref[...], g_ref[...], sc_ref[...], sh_ref[...]).astype(BF16)

    o_ref[...] = jnp.dot(h_ref[...], w_ref[...], preferred_element_type=F32) + b_ref[...]


def _in_proj(x, g_mix, scale1, shift1, w_in_p, b_in_p):
    bsz, s, d = x.shape
    n = w_in_p.shape[1]
    tm = min(1024, s)
    tn = 512
    return pl.pallas_call(
        _inproj_kernel,
        grid=(bsz, s // tm, n // tn),
        in_specs=[pl.BlockSpec((None, tm, d), lambda b, i, j: (b, i, 0)),
                  pl.BlockSpec((1, d), lambda b, i, j: (0, 0)),
                  pl.BlockSpec((None, 1, d), lambda b, i, j: (b, 0, 0)),
                  pl.BlockSpec((None, 1, d), lambda b, i, j: (b, 0, 0)),
                  pl.BlockSpec((d, tn), lambda b, i, j: (0, j)),
                  pl.BlockSpec((1, tn), lambda b, i, j: (0, j))],
        out_specs=pl.BlockSpec((None, tm, tn), lambda b, i, j: (b, i, j)),
        out_shape=jax.ShapeDtypeStruct((bsz, s, n), F32),
        scratch_shapes=[pltpu.VMEM((tm, d), BF16)],
        compiler_params=_cparams(("parallel", "parallel", "arbitrary")),
        name="in_proj",
    )(x, g_mix.reshape(1, d), scale1, shift1, w_in_p, b_in_p.reshape(1, n))


def _rope_kernel(pos_ref, freq_ref, c_ref, s1_ref, s2_ref):
    ang = pos_ref[...] * freq_ref[...]
    cs = jnp.cos(ang)
    sn = jnp.sin(ang)
    d = lax.broadcasted_iota(jnp.int32, ang.shape, 1) & (HEAD_DIM - 1)
    c_ref[...] = cs
    s1_ref[...] = jnp.where(d < ROPE_DIMS // 2, -sn, 0.0)
    s2_ref[...] = jnp.where((d >= ROPE_DIMS // 2) & (d < ROPE_DIMS), sn, 0.0)


def _rope_tables(positions):
    bsz, s = positions.shape
    t = bsz * s
    tm = min(1024, t)
    inv_freq = ROPE_THETA ** (-(jnp.arange(0, ROPE_DIMS, 2, dtype=F32) / ROPE_DIMS))
    d = jnp.arange(LANES) % HEAD_DIM
    freq = jnp.where(d < ROPE_DIMS, inv_freq[d % (ROPE_DIMS // 2)], 0.0).astype(F32).reshape(1, LANES)
    pos = positions.astype(F32).reshape(t, 1)
    sds = jax.ShapeDtypeStruct((t, LANES), F32)
    outs = pl.pallas_call(
        _rope_kernel,
        grid=(t // tm,),
        in_specs=[pl.BlockSpec((tm, 1), lambda i: (i, 0)),
                  pl.BlockSpec((1, LANES), lambda i: (0, 0))],
        out_specs=[pl.BlockSpec((tm, LANES), lambda i: (i, 0))] * 3,
        out_shape=[sds, sds, sds],
        compiler_params=_cparams(("parallel",)),
        name="rope_tables",
    )(pos, freq)
    return [o.reshape(bsz, s, LANES) for o in outs]


def _rope_apply(t, c, s1, s2):
    outs = []
    for j in range(t.shape[1] // LANES):
        tj = t[:, j * LANES:(j + 1) * LANES]
        outs.append(tj * c + pltpu.roll(tj, LANES - ROPE_DIMS // 2, 1) * s1
                    + pltpu.roll(tj, ROPE_DIMS // 2, 1) * s2)
    return jnp.concatenate(outs, axis=1)


def _attn_kernel(sink_ref, q_ref, kvc_ref, kvp_ref, cq_ref, s1q_ref, s2q_ref,
                 cp_ref, s1p_ref, s2p_ref, bm_ref, o_ref):
    n = pl.program_id(1)
    cq, s1q, s2q = cq_ref[...], s1q_ref[...], s2q_ref[...]
    q = (_rope_apply(q_ref[...], cq, s1q, s2q) * (HEAD_DIM ** -0.5)).astype(BF16)
    kvc = kvc_ref[...]
    kvp = kvp_ref[...]
    kc = _rope_apply(kvc[:, :KV_WIDTH], cq, s1q, s2q)
    kp = _rope_apply(kvp[:, :KV_WIDTH], cp_ref[...], s1p_ref[...], s2p_ref[...])
    kband = jnp.concatenate([kp, kc], axis=0).astype(BF16)
    vband = jnp.concatenate([kvp[:, KV_WIDTH:], kvc[:, KV_WIDTH:]], axis=0).astype(BF16)
    bm = bm_ref[...]
    kexp = jnp.concatenate([kband] * N_KV_HEADS, axis=0) * bm
    vexp = jnp.concatenate([vband] * N_KV_HEADS, axis=0) * bm
    row = lax.broadcasted_iota(jnp.int32, (WINDOW, 2 * WINDOW), 0)
    col = lax.broadcasted_iota(jnp.int32, (WINDOW, 2 * WINDOW), 1)
    diff = row + WINDOW - col
    valid = (diff >= 0) & (diff < WINDOW) & ((col >= WINDOW) | (n > 0))
    rep = N_Q_HEADS // N_KV_HEADS
    kw = 2 * WINDOW
    for r in range(rep):
        qr = q[:, r * KV_WIDTH:(r + 1) * KV_WIDTH]
        sc = lax.dot_general(qr, kexp, (((1,), (1,)), ((), ())), preferred_element_type=F32)
        ps = []
        for g in range(N_KV_HEADS):
            s = jnp.where(valid, sc[:, g * kw:(g + 1) * kw], MASK_VALUE)
            sink = sink_ref[g * rep + r]
            m = jnp.maximum(jnp.max(s, axis=-1, keepdims=True), sink)
            p = jnp.exp(s - m)
            l = jnp.sum(p, axis=-1, keepdims=True) + jnp.exp(sink - m)
            ps.append((p / l).astype(BF16))
        pcat = jnp.concatenate(ps, axis=1)
        o = jnp.dot(pcat, vexp, preferred_element_type=F32)
        o_ref[:, r * KV_WIDTH:(r + 1) * KV_WIDTH] = o.astype(BF16)


def _attention(proj, sinks, ctab, s1tab, s2tab):
    bsz, s, _ = proj.shape
    nb = s // WINDOW
    g_row = jnp.arange(N_KV_HEADS * 2 * WINDOW) // (2 * WINDOW)
    g_col = jnp.arange(KV_WIDTH) // HEAD_DIM
    bm = (g_row[:, None] == g_col[None, :]).astype(BF16)
    cur = lambda b, n: (b, n, 0)
    prev = lambda b, n: (b, jnp.maximum(n - 1, 0), 0)
    kvw = 2 * KV_WIDTH
    tab = pl.BlockSpec((None, WINDOW, LANES), cur)
    tabp = pl.BlockSpec((None, WINDOW, LANES), prev)
    return pl.pallas_call(
        _attn_kernel,
        grid=(bsz, nb),
        in_specs=[pl.BlockSpec(memory_space=pltpu.SMEM),
                  pl.BlockSpec((None, WINDOW, ATTN_WIDTH), cur),
                  pl.BlockSpec((None, WINDOW, kvw), lambda b, n: (b, n, COL_KV // kvw)),
                  pl.BlockSpec((None, WINDOW, kvw), lambda b, n: (b, jnp.maximum(n - 1, 0), COL_KV // kvw)),
                  tab, tab, tab, tabp, tabp, tabp,
                  pl.BlockSpec((N_KV_HEADS * 2 * WINDOW, KV_WIDTH), lambda b, n: (0, 0))],
        out_specs=pl.BlockSpec((None, WINDOW, ATTN_WIDTH), cur),
        out_shape=jax.ShapeDtypeStruct((bsz, s, ATTN_WIDTH), BF16),
        compiler_params=_cparams(("parallel", "parallel")),
        name="attention",
    )(sinks, proj, proj, proj, ctab, s1tab, s2tab, ctab, s1tab, s2tab, bm)


def _ssm_prep_kernel(are_ref, aim_ref, ldt_ref, bre_ref, bim_ref,
                     abre_ref, abim_ref, bbre_ref, bbim_ref):
    dt = jnp.exp(ldt_ref[...])
    lam_re = jnp.minimum(are_ref[...], -1e-4)
    lam_im = aim_ref[...]
    mag = jnp.exp(lam_re * dt)
    ab_re = mag * jnp.cos(lam_im * dt)
    ab_im = mag * jnp.sin(lam_im * dt)
    den = lam_re * lam_re + lam_im * lam_im
    num_re = ab_re - 1.0
    z_re = (num_re * lam_re + ab_im * lam_im) / den
    z_im = (ab_im * lam_re - num_re * lam_im) / den
    abre_ref[...] = ab_re
    abim_ref[...] = ab_im
    bre = bre_ref[...]
    bim = bim_ref[...]
    bbre_ref[...] = z_re * bre - z_im * bim
    bbim_ref[...] = z_re * bim + z_im * bre


def _ssm_prep(a_re, a_im, log_dt, b_re, b_im):
    g, p, h = b_re.shape
    a3 = jax.ShapeDtypeStruct((g, 1, p), F32)
    b3 = jax.ShapeDtypeStruct((g, h, p), F32)
    return pl.pallas_call(
        _ssm_prep_kernel,
        out_shape=[a3, a3, b3, b3],
        name="ssm_prep",
    )(a_re.reshape(g, 1, p), a_im.reshape(g, 1, p), log_dt.reshape(g, 1, 1),
      jnp.swapaxes(b_re, 1, 2), jnp.swapaxes(b_im, 1, 2))


def _ssm_kernel(u_ref, wb_ref, wc_ref, are_ref, aim_ref, d_ref, wg_ref, bg_ref, o_ref,
                xre, xim, sre, sim, *, n_batch, t_chunk, lane_chunk):
    @pl.when(pl.program_id(0) == 0)
    def _():
        sre[...] = jnp.zeros_like(sre)
        sim[...] = jnp.zeros_like(sim)

    n_slices = SSM_WIDTH // LANES
    sw = SSM_SLICE_GROUPS * SSM_STATE
    u = u_ref[...]
    ub = u.astype(BF16)
    for s in range(n_slices):
        bu = jnp.dot(ub[:, s * LANES:(s + 1) * LANES], wb_ref[s], preferred_element_type=F32)
        xre[:, s * sw:(s + 1) * sw] = bu[:, :sw]
        xim[:, s * sw:(s + 1) * sw] = bu[:, sw:]

    for lc in range(STATE_LANES // lane_chunk):
        sl = slice(lc * lane_chunk, (lc + 1) * lane_chunk)
        ar = jnp.broadcast_to(are_ref[:, sl], (n_batch, lane_chunk))
        ai = jnp.broadcast_to(aim_ref[:, sl], (n_batch, lane_chunk))

        def body(t, carry, sl=sl, ar=ar, ai=ai):
            cr, ci = carry
            off = pl.multiple_of(t * n_batch, n_batch)
            nr = ar * cr - ai * ci + xre[pl.ds(off, n_batch), sl]
            ni = ar * ci + ai * cr + xim[pl.ds(off, n_batch), sl]
            xre[pl.ds(off, n_batch), sl] = nr
            xim[pl.ds(off, n_batch), sl] = ni
            return nr, ni

        cr, ci = lax.fori_loop(0, t_chunk, body, (sre[:, sl], sim[:, sl]), unroll=8)
        sre[:, sl] = cr
        sim[:, sl] = ci

    ys = []
    for s in range(n_slices):
        xs = jnp.concatenate([xre[:, s * sw:(s + 1) * sw], xim[:, s * sw:(s + 1) * sw]], axis=1)
        ys.append(jnp.dot(xs.astype(BF16), wc_ref[s], preferred_element_type=F32))
    y = jnp.concatenate(ys, axis=1) + d_ref[...] * u
    z = jax.nn.gelu(y)
    gl = jnp.dot(z.astype(BF16), wg_ref[...], preferred_element_type=F32) + bg_ref[...]
    o_ref[...] = (z * jax.nn.sigmoid(gl)).astype(BF16)


def _ssm(u_tm, n_batch, wb, wc, ab_re, ab_im, d_skip, w_glu, b_glu):
    rows, w = u_tm.shape
    s = rows // n_batch
    t_chunk = min(64, s)
    r = t_chunk * n_batch
    full = lambda shape: pl.BlockSpec(shape, lambda i: (0,) * len(shape))
    kern = functools.partial(_ssm_kernel, n_batch=n_batch, t_chunk=t_chunk, lane_chunk=512)
    return pl.pallas_call(
        kern,
        grid=(s // t_chunk,),
        in_specs=[pl.BlockSpec((r, w), lambda i: (i, 0)),
                  full(wb.shape), full(wc.shape), full((1, STATE_LANES)), full((1, STATE_LANES)),
                  full((1, w)), full((w, w)), full((1, w))],
        out_specs=pl.BlockSpec((r, w), lambda i: (i, 0)),
        out_shape=jax.ShapeDtypeStruct((rows, w), BF16),
        scratch_shapes=[pltpu.VMEM((r, STATE_LANES), F32), pltpu.VMEM((r, STATE_LANES), F32),
                        pltpu.VMEM((n_batch, STATE_LANES), F32), pltpu.VMEM((n_batch, STATE_LANES), F32)],
        compiler_params=_cparams(("arbitrary",)),
        name="ssm",
    )(u_tm, wb, wc, ab_re, ab_im, d_skip.reshape(1, w), w_glu, b_glu.reshape(1, w))


def _merge_kernel(attn_ref, sso_ref, ga_ref, gs_ref, x_ref, g1_ref, gf_ref, sc2_ref, sh2_ref,
                  wab_ref, wsb_ref, wo_ref, x1_ref, hi_ref, lo_ref):
    ya = jnp.dot(attn_ref[...], wab_ref[...], preferred_element_type=F32)
    ys = jnp.dot(sso_ref[...], wsb_ref[...], preferred_element_type=F32)
    merged = jax.nn.sigmoid(ga_ref[...]) * ya + jax.nn.sigmoid(gs_ref[...]) * ys
    mo = jnp.dot(merged.astype(BF16), wo_ref[...], preferred_element_type=F32)
    x1 = x_ref[...] + g1_ref[...] * mo
    x1_ref[...] = x1
    h2t = _rms_mod(x1, gf_ref[...], sc2_ref[...], sh2_ref[...]).T
    hi = h2t.astype(BF16)
    hi_ref[...] = hi
    lo_ref[...] = (h2t - hi.astype(F32)).astype(BF16)


def _merge(attn, sso, proj, x, gate1, g_ffn, scale2, shift2, wab, wsb, wo):
    bsz, s, d = x.shape
    tm = min(256, s)
    nt = s // tm
    row = lambda w, cb: pl.BlockSpec((None, tm, w), lambda b, i: (b, i, cb))
    per_b = pl.BlockSpec((None, 1, d), lambda b, i: (b, 0, 0))
    const = lambda shape: pl.BlockSpec(shape, lambda b, i: (0, 0), pipeline_mode=pl.Buffered(1))
    tsds = jax.ShapeDtypeStruct((d, bsz * s), BF16)
    tspec = pl.BlockSpec((d, tm), lambda b, i: (0, b * nt + i))
    return pl.pallas_call(
        _merge_kernel,
        grid=(bsz, nt),
        in_specs=[row(ATTN_WIDTH, 0), row(SSM_WIDTH, 0), row(d, COL_GA // d), row(d, COL_GS // d),
                  row(d, 0), per_b, pl.BlockSpec((1, d), lambda b, i: (0, 0)), per_b, per_b,
                  const(wab.shape), const(wsb.shape), const(wo.shape)],
        out_specs=[row(d, 0), tspec, tspec],
        out_shape=[jax.ShapeDtypeStruct((bsz, s, d), F32), tsds, tsds],
        compiler_params=_cparams(("parallel", "parallel")),
        name="merge",
    )(attn, sso, proj, proj, x, gate1, g_ffn.reshape(1, d), scale2, shift2, wab, wsb, wo)


def _dot3(a_hi, a_lo, b_hi, b_lo):
    return (jnp.dot(a_hi, b_hi, preferred_element_type=F32)
            + jnp.dot(a_hi, b_lo, preferred_element_type=F32)
            + jnp.dot(a_lo, b_hi, preferred_element_type=F32))


def _top16(v):
    idx = lax.broadcasted_iota(jnp.int32, v.shape, 0)
    rank = jnp.full(v.shape, 100.0, F32)
    vals = []
    for it in range(PEER_TOPK):
        m = jnp.max(v, axis=0, keepdims=True)
        istar = jnp.min(jnp.where(v == m, idx, PEER_NKEYS), axis=0, keepdims=True)
        sel = idx == istar
        rank = jnp.where(sel, float(it + 1), rank)
        v = jnp.where(sel, -jnp.inf, v)
        vals.append(m)
    return jnp.concatenate(vals, axis=0), rank


def _staircase(a, b):
    lanes = a.shape[1]
    rowi = lax.broadcasted_iota(jnp.int32, (8, lanes), 0)
    cs, codes = [], []
    for q in range(PEER_TOPK):
        nq = PEER_TOPK // (q + 1)
        for blk in range((nq + 7) // 8):
            p = rowi + 8 * blk
            c = a[8 * blk:8 * blk + 8, :] + b[q:q + 1, :]
            cs.append(jnp.where(p < nq, c, -jnp.inf))
            codes.append(p * PEER_TOPK + q)
    cand = jnp.concatenate(cs, axis=0)
    code = jnp.concatenate(codes, axis=0)
    row16 = lax.broadcasted_iota(jnp.int32, (PEER_TOPK, lanes), 0)
    cnt = jnp.zeros((PEER_TOPK, lanes), F32)
    m0 = a[0:1, :] + b[0:1, :]
    z = jnp.zeros((1, lanes), F32)
    for _ in range(PEER_TOPK):
        m = jnp.max(cand, axis=0, keepdims=True)
        cstar = jnp.min(jnp.where(cand == m, code, 1 << 20), axis=0, keepdims=True)
        cand = jnp.where(code == cstar, -jnp.inf, cand)
        z = z + jnp.exp(m - m0)
        cnt = cnt + jnp.where(row16 == (cstar >> 4), 1.0, 0.0)
    return cnt, z


def _batcher_pairs(n):
    pairs = []

    def merge(lo, hi, r):
        step = r * 2
        if step < hi - lo:
            merge(lo, hi, step)
            merge(lo + r, hi, step)
            pairs.extend((i, i + r) for i in range(lo + r, hi - r, step))
        else:
            pairs.append((lo, lo + r))

    def sort(lo, hi):
        if hi - lo >= 1:
            mid = lo + (hi - lo) // 2
            sort(lo, mid)
            sort(mid + 1, hi)
            merge(lo, hi, 1)

    sort(0, n - 1)
    return pairs


_SORT16 = _batcher_pairs(PEER_TOPK)
SUBLANES = 8


def _cmpx(v, i, j):
    hi, lo = jnp.maximum(v[i], v[j]), jnp.minimum(v[i], v[j])
    v[i], v[j] = hi, lo


def _sub_allreduce(x, op):
    for shift in (4, 2, 1):
        x = op(x, pltpu.roll(x, shift, 0))
    return x


def _sorted_top16(s):
    v = [s[SUBLANES * k:SUBLANES * (k + 1), :] for k in range(PEER_TOPK)]
    for i, j in _SORT16:
        _cmpx(v, i, j)
    for shift in (4, 2, 1):
        m = [jnp.maximum(v[k], pltpu.roll(v[PEER_TOPK - 1 - k], shift, 0)) for k in range(PEER_TOPK)]
        for d in (8, 4, 2, 1):
            for i in range(PEER_TOPK):
                if not i & d:
                    _cmpx(m, i, i + d)
        v = m
    return v


def _count_sorted(b, passes):
    t3 = passes(b[7])
    t2 = passes(jnp.where(t3, b[11], b[3]))
    t1 = passes(jnp.where(t3, jnp.where(t2, b[13], b[9]), jnp.where(t2, b[5], b[1])))
    c0 = jnp.where(t3,
                   jnp.where(t2, jnp.where(t1, b[14], b[12]), jnp.where(t1, b[10], b[8])),
                   jnp.where(t2, jnp.where(t1, b[6], b[4]), jnp.where(t1, b[2], b[0])))
    t0 = passes(c0)
    cnt = (jnp.where(t3, 8.0, 0.0) + jnp.where(t2, 4.0, 0.0)
           + jnp.where(t1, 2.0, 0.0) + jnp.where(t0, 1.0, 0.0))
    return jnp.where(passes(b[15]), 16.0, cnt)


def _route_fast(s0, s1):
    a = _sorted_top16(s0)
    b = _sorted_top16(s1)
    sub = lax.broadcasted_iota(jnp.int32, a[0].shape, 0)
    ninf = -jnp.inf

    def pick(vs, base):
        acc = vs[base + SUBLANES - 1]
        for r in range(SUBLANES - 2, -1, -1):
            acc = jnp.where(sub == r, vs[base + r], acc)
        return acc

    a_lo, a_hi, b_lo, b_hi = pick(a, 0), pick(a, 8), pick(b, 0), pick(b, 8)
    ge2 = sub >= 2
    packs = [a[0] + b_lo, a[0] + b_hi, a[1] + b_lo, a_hi + b[0],
             jnp.where(ge2, a_lo + b[0], ninf), jnp.where(ge2, a_lo + b[1], ninf),
             jnp.where(ge2 & (sub <= 4), a_lo + b[2], ninf),
             jnp.where((sub == 3) | (sub == 4), a[2] + b_lo, jnp.where(sub == 5, a[3] + b[3], ninf))]
    work = list(packs)
    tau = None
    for it in range(PEER_TOPK):
        m = work[0]
        for w in work[1:]:
            m = jnp.maximum(m, w)
        tau = _sub_allreduce(m, jnp.maximum)
        if it < PEER_TOPK - 1:
            work = [jnp.where(w == tau, ninf, w) for w in work]
    m0 = a[0] + b[0]
    z = None
    for p in packs:
        term = jnp.where(p >= tau, jnp.exp(p - m0), 0.0)
        z = term if z is None else z + term
    z = _sub_allreduce(z, jnp.add)

    g, l, e1, r1 = [], [], [], []
    lsum = c16 = rsum = None
    for k in range(PEER_NKEYS // SUBLANES):
        s0v = s0[SUBLANES * k:SUBLANES * (k + 1), :]
        s1v = s1[SUBLANES * k:SUBLANES * (k + 1), :]
        lv = _count_sorted(b, lambda x, s0v=s0v: (s0v + x) >= tau)
        rv = _count_sorted(b, lambda x, s1v=s1v: x > s1v) + 1.0
        in16 = rv <= float(PEER_TOPK)
        one = jnp.where(in16, 1.0, 0.0)
        rin = jnp.where(in16, rv, 0.0)
        lsum = lv if lsum is None else lsum + lv
        c16 = one if c16 is None else c16 + one
        rsum = rin if rsum is None else rsum + rin
        g.append(jnp.exp(s0v - a[0]) / z)
        l.append(lv)
        e1.append(jnp.exp(s1v - b[0]))
        r1.append(rv)
    lsum = _sub_allreduce(lsum, jnp.add)
    c16 = _sub_allreduce(c16, jnp.add)
    rsum = _sub_allreduce(rsum, jnp.add)
    n = float(PEER_TOPK)
    clean = (lsum == n) & (c16 == n) & (rsum == n * (n + 1.0) / 2.0)
    return g, l, e1, r1, jnp.where(clean, 0.0, 1.0)


def _route_exact(s0, s1):
    a, rank0 = _top16(s0)
    b, rank1 = _top16(s1)
    cnt, z = _staircase(a, b)
    lsel = jnp.zeros(s0.shape, F32)
    for p in range(PEER_TOPK):
        lsel = lsel + jnp.where(rank0 == float(p + 1), cnt[p:p + 1, :], 0.0)
    return jnp.exp(s0 - a[0:1, :]) / z, lsel, jnp.exp(s1 - b[0:1, :]), rank1


def _kw_kernel(kh_ref, kl_ref, wh_ref, wl_ref, o_ref):
    o_ref[...] = _dot3(kh_ref[...], kl_ref[...], wh_ref[...], wl_ref[...])


def _key_query_weights(k_hi, k_lo, wq_hi, wq_lo):
    n, nk, hd = k_hi.shape
    d = wq_hi.shape[-1]
    ks = pl.BlockSpec((None, nk, hd), lambda i: (i, 0, 0))
    ws = pl.BlockSpec((None, hd, d), lambda i: (i, 0, 0))
    return pl.pallas_call(
        _kw_kernel,
        grid=(n,),
        in_specs=[ks, ks, ws, ws],
        out_specs=pl.BlockSpec((None, nk, d), lambda i: (i, 0, 0)),
        out_shape=jax.ShapeDtypeStruct((n, nk, d), F32),
        compiler_params=_cparams(("parallel",)),
        name="peer_kw",
    )(k_hi, k_lo, wq_hi, wq_lo)


def _route_kernel(xh_ref, xl_ref, wh_ref, wl_ref, g_ref, l_ref, e1_ref, r1_ref, sc_ref):
    sc_ref[...] = _dot3(wh_ref[...], wl_ref[...], xh_ref[...], xl_ref[...])
    ng = sc_ref.shape[1] // LANES

    def lane_group(idx, carry):
        h = idx // ng
        off = pl.multiple_of((idx % ng) * LANES, LANES)
        row = pl.multiple_of(h * (2 * PEER_NKEYS), 2 * PEER_NKEYS)
        s0 = sc_ref[pl.ds(row, PEER_NKEYS), pl.ds(off, LANES)]
        s1 = sc_ref[pl.ds(row + PEER_NKEYS, PEER_NKEYS), pl.ds(off, LANES)]
        g, l, e1, r1, bad = _route_fast(s0, s1)
        cat = lambda pieces: jnp.concatenate(pieces, axis=0)
        g_ref[h, :, pl.ds(off, LANES)] = cat(g)
        l_ref[h, :, pl.ds(off, LANES)] = cat(l)
        e1_ref[h, :, pl.ds(off, LANES)] = cat(e1).astype(BF16)
        r1_ref[h, :, pl.ds(off, LANES)] = cat(r1).astype(BF16)

        @pl.when(jnp.max(bad) > 0.0)
        def _():
            gx, lx, e1x, r1x = _route_exact(s0, s1)
            g_ref[h, :, pl.ds(off, LANES)] = gx
            l_ref[h, :, pl.ds(off, LANES)] = lx
            e1_ref[h, :, pl.ds(off, LANES)] = e1x.astype(BF16)
            r1_ref[h, :, pl.ds(off, LANES)] = r1x.astype(BF16)

        return carry

    lax.fori_loop(0, PEER_HEADS * ng, lane_group, 0)


def _peer_route(h2t_hi, h2t_lo, kw_hi, kw_lo):
    d, t = h2t_hi.shape
    tn = min(256, t)
    nq = kw_hi.shape[0]
    xs = pl.BlockSpec((d, tn), lambda i: (0, i))
    ws = pl.BlockSpec((nq, d), lambda i: (0, 0), pipeline_mode=pl.Buffered(1))
    os_ = pl.BlockSpec((PEER_HEADS, PEER_NKEYS, tn), lambda i: (0, 0, i))
    f = jax.ShapeDtypeStruct((PEER_HEADS, PEER_NKEYS, t), F32)
    bf = jax.ShapeDtypeStruct((PEER_HEADS, PEER_NKEYS, t), BF16)
    return pl.pallas_call(
        _route_kernel,
        grid=(t // tn,),
        in_specs=[xs, xs, ws, ws],
        out_specs=[os_, os_, os_, os_],
        out_shape=[f, f, bf, bf],
        scratch_shapes=[pltpu.VMEM((nq, tn), F32)],
        compiler_params=_cparams(("parallel",)),
        name="peer_route",
    )(h2t_hi, h2t_lo, kw_hi, kw_lo)


def _dense_kernel(h_ref, dn_ref, up_ref, g_ref, l_ref, e1_ref, r1_ref, o_ref, at0_ref, at1_ref, *, n_i, n_e):
    s = pl.program_id(0)
    prev_block = jnp.maximum(s - 1, 0) % n_e

    @pl.when(s == 0)
    def _():
        at1_ref[...] = jnp.zeros_like(at1_ref)

    @pl.when(prev_block == 0)
    def _():
        o_ref[...] = jnp.zeros_like(o_ref)

    def step(new_ref, prev_ref):
        new_ref[...] = jnp.dot(dn_ref[...], h_ref[...], preferred_element_type=F32)
        live = s > 0
        wgs = []
        for il in range(n_i):
            w = None
            for h in range(PEER_HEADS):
                gi = g_ref[h, il:il + 1, :].astype(BF16)
                li = l_ref[h, il:il + 1, :].astype(BF16)
                term = jnp.where(r1_ref[h] <= li, e1_ref[h] * gi, jnp.zeros((), BF16))
                w = term if w is None else w + term
            act = jax.nn.gelu(prev_ref[il * PEER_NKEYS:(il + 1) * PEER_NKEYS, :]).astype(BF16)
            wgs.append(jnp.where(live, act * w, jnp.zeros((), BF16)))
        wg = jnp.concatenate(wgs, axis=0)
        o_ref[...] += jnp.dot(up_ref[...], wg, preferred_element_type=F32)

    @pl.when(s % 2 == 0)
    def _():
        step(at0_ref, at1_ref)

    @pl.when(s % 2 == 1)
    def _():
        step(at1_ref, at0_ref)


def _peer_dense(h2t_hi, down_bf, up_t_bf, g, l, e1, r1):
    d, t = h2t_hi.shape
    ne = down_bf.shape[0]
    tm = min(512, t)
    en = 1024
    n_e = ne // en
    n_i = en // PEER_NKEYS
    n_steps = (t // tm) * n_e
    cur = lambda s: jnp.minimum(s, n_steps - 1)
    prv = lambda s: jnp.maximum(s - 1, 0)
    rt = lambda n: pl.BlockSpec((PEER_HEADS, n, tm),
                                lambda s: (0, prv(s) % n_e if n == n_i else 0, prv(s) // n_e))
    return pl.pallas_call(
        functools.partial(_dense_kernel, n_i=n_i, n_e=n_e),
        grid=(n_steps + 1,),
        in_specs=[pl.BlockSpec((d, tm), lambda s: (0, cur(s) // n_e)),
                  pl.BlockSpec((en, d), lambda s: (cur(s) % n_e, 0)),
                  pl.BlockSpec((d, en), lambda s: (0, prv(s) % n_e)),
                  rt(n_i), rt(n_i), rt(PEER_NKEYS), rt(PEER_NKEYS)],
        out_specs=pl.BlockSpec((d, tm), lambda s: (0, prv(s) // n_e)),
        out_shape=jax.ShapeDtypeStruct((d, t), F32),
        scratch_shapes=[pltpu.VMEM((en, tm), F32), pltpu.VMEM((en, tm), F32)],
        compiler_params=_cparams(("arbitrary",)),
        name="peer_dense",
    )(h2t_hi, down_bf, up_t_bf, g, l, e1, r1)


def _final_kernel(x1_ref, pt_ref, g2_ref, gf_ref, o_ref):
    x2 = x1_ref[...] + g2_ref[...] * pt_ref[...].T
    ms = jnp.mean(x2 * x2, axis=-1, keepdims=True)
    o_ref[...] = x2 * lax.rsqrt(ms + RMS_EPS) * gf_ref[...]


def _final(x1, peer_t, gate2, g_final):
    bsz, s, d = x1.shape
    tm = min(512, s)
    nt = s // tm
    row = pl.BlockSpec((None, tm, d), lambda b, i: (b, i, 0))
    return pl.pallas_call(
        _final_kernel,
        grid=(bsz, nt),
        in_specs=[row, pl.BlockSpec((d, tm), lambda b, i: (0, b * nt + i)),
                  pl.BlockSpec((None, 1, d), lambda b, i: (b, 0, 0)),
                  pl.BlockSpec((1, d), lambda b, i: (0, 0))],
        out_specs=row,
        out_shape=jax.ShapeDtypeStruct((bsz, s, d), F32),
        compiler_params=_cparams(("parallel", "parallel")),
        name="final",
    )(x1, peer_t, gate2, g_final.reshape(1, d))


def _head_perm(w, axis):
    rep = N_Q_HEADS // N_KV_HEADS
    shape = w.shape
    w = w.reshape(shape[:axis] + (N_KV_HEADS, rep, HEAD_DIM) + shape[axis + 1:])
    return jnp.swapaxes(w, axis, axis + 1).reshape(shape)


def _split_bf16(w):
    hi = w.astype(BF16)
    return hi, (w - hi.astype(F32)).astype(BF16)


def _layer(x, c, positions, w_ada, b_ada, g_mix, w_in, b_in, attn_sinks, w_attn_branch,
           ssm_A_re, ssm_A_im, ssm_log_dt, ssm_B_re, ssm_B_im, ssm_C_re, ssm_C_im, ssm_D,
           w_glu, b_glu, w_ssm_branch, w_out, g_ffn, w_query, sub_keys, expert_down, expert_up):
    bsz, s, d = x.shape
    mod = _ada_mod(c, w_ada, b_ada)
    shift1, scale1, gate1, shift2, scale2, gate2 = [m.reshape(bsz, 1, d) for m in jnp.split(mod, 6, axis=-1)]

    o1, o3, o4 = ATTN_WIDTH, ATTN_WIDTH + 2 * KV_WIDTH, ATTN_WIDTH + 2 * KV_WIDTH + SSM_WIDTH
    permute = lambda w: jnp.concatenate(
        [_head_perm(w[..., :o1], w.ndim - 1), w[..., o3:o4], w[..., o4:], w[..., o1:o3]], axis=-1)
    proj = _in_proj(x, g_mix, scale1, shift1, permute(w_in).astype(BF16), permute(b_in))

    ctab, s1tab, s2tab = _rope_tables(positions)
    attn = _attention(proj, attn_sinks, ctab, s1tab, s2tab)

    ab_re, ab_im, bb_re, bb_im = _ssm_prep(ssm_A_re, ssm_A_im, ssm_log_dt, ssm_B_re, ssm_B_im)
    ns = N_SSM_GROUPS // SSM_SLICE_GROUPS
    eye = jnp.eye(SSM_SLICE_GROUPS, dtype=F32)
    shp = (ns, SSM_SLICE_GROUPS, SSM_GROUP, SSM_STATE)
    blk_b = lambda m: jnp.einsum("sghp,gk->sghkp", m.reshape(shp), eye).reshape(ns, LANES, -1)
    blk_c = lambda m: jnp.einsum("sghp,gk->sgpkh", m.reshape(shp), eye).reshape(ns, -1, LANES)
    wb = jnp.concatenate([blk_b(bb_re), blk_b(bb_im)], axis=-1).astype(BF16)
    wc = jnp.concatenate([blk_c(ssm_C_re), -blk_c(ssm_C_im)], axis=1).astype(BF16)
    u_tm = jnp.swapaxes(proj[:, :, COL_U:COL_U + SSM_WIDTH], 0, 1).reshape(s * bsz, SSM_WIDTH)
    sso_tm = _ssm(u_tm, bsz, wb, wc, ab_re.reshape(1, -1), ab_im.reshape(1, -1), ssm_D,
                  w_glu.astype(BF16), b_glu)
    sso = jnp.swapaxes(sso_tm.reshape(s, bsz, SSM_WIDTH), 0, 1)

    x1, h2t_hi, h2t_lo = _merge(attn, sso, proj, x, gate1, g_ffn, scale2, shift2,
                                _head_perm(w_attn_branch, 0).astype(BF16), w_ssm_branch.astype(BF16),
                                w_out.astype(BF16))

    wq_hi, wq_lo = _split_bf16(w_query.T.reshape(2 * PEER_HEADS, PEER_HALF, d))
    k_hi, k_lo = _split_bf16(sub_keys.reshape(2 * PEER_HEADS, PEER_NKEYS, PEER_HALF))
    kw_hi, kw_lo = _split_bf16(_key_query_weights(k_hi, k_lo, wq_hi, wq_lo).reshape(-1, d))
    g, l, e1, r1 = _peer_route(h2t_hi, h2t_lo, kw_hi, kw_lo)
    peer_t = _peer_dense(h2t_hi, expert_down.astype(BF16), expert_up.T.astype(BF16), g, l, e1, r1)
    return x1, peer_t, gate2


def kernel(x, c, positions, w_ada, b_ada, g_mix, w_in, b_in, attn_sinks, w_attn_branch, ssm_A_re, ssm_A_im, ssm_log_dt, ssm_B_re, ssm_B_im, ssm_C_re, ssm_C_im, ssm_D, w_glu, b_glu, w_ssm_branch, w_out, g_ffn, w_query, sub_keys, expert_down, expert_up, g_final):
    depth = w_ada.shape[0]
    assert depth == 1
    x1, peer_t, gate2 = _layer(x, c, positions, w_ada[0], b_ada[0], g_mix[0], w_in[0], b_in[0],
                               attn_sinks[0], w_attn_branch[0], ssm_A_re[0], ssm_A_im[0], ssm_log_dt[0],
                               ssm_B_re[0], ssm_B_im[0], ssm_C_re[0], ssm_C_im[0], ssm_D[0], w_glu[0],
                               b_glu[0], w_ssm_branch[0], w_out[0], g_ffn[0], w_query[0], sub_keys[0],
                               expert_down[0], expert_up[0])
    return _final(x1, peer_t, gate2, g_final)
```

```python
import functools
import math

import jax
import jax.numpy as jnp
from jax import lax
from jax.experimental import pallas as pl
from jax.experimental.pallas import tpu as pltpu

F32 = jnp.float32
BF16 = jnp.bfloat16

D_MODEL = 2048
ATTN_WIDTH = 1024
SSM_WIDTH = 1024
HEAD_DIM = 64
N_Q_HEADS = 16
N_KV_HEADS = 4
KV_WIDTH = 256
WINDOW = 128
ROPE_THETA = 500000.0
ROPE_DIMS = 16
SSM_GROUP = 16
N_SSM_GROUPS = 64
SSM_STATE = 64
STATE_LANES = N_SSM_GROUPS * SSM_STATE
PEER_HEADS = 8
PEER_NKEYS = 128
PEER_N = PEER_NKEYS * PEER_NKEYS
PEER_HALF = 128
PEER_TOPK = 16
RMS_EPS = 1e-6
MASK_VALUE = -1e30
IN_COLS = ATTN_WIDTH + 2 * KV_WIDTH + SSM_WIDTH + 2 * D_MODEL

LANES = 128
SSM_SLICE_GROUPS = 8
VMEM_LIMIT = 56 * 1024 * 1024

COL_Q = 0
COL_U = ATTN_WIDTH
COL_GA = COL_U + SSM_WIDTH
COL_GS = COL_GA + D_MODEL
COL_KV = COL_GS + D_MODEL


def _cparams(sem):
    return pltpu.CompilerParams(dimension_semantics=sem, vmem_limit_bytes=VMEM_LIMIT)


def _rms_mod(x, g, scale, shift):
    ms = jnp.mean(x * x, axis=-1, keepdims=True)
    y = x * lax.rsqrt(ms + RMS_EPS) * g
    return y * (1.0 + scale) + shift


def _ada_kernel(c_ref, w_ref, b_ref, o_ref):
    c = c_ref[...]
    sc = c * jax.nn.sigmoid(c)
    o_ref[...] = jnp.dot(sc, w_ref[...], preferred_element_type=F32,
                         precision=lax.Precision.HIGHEST) + b_ref[...]


def _ada_mod(c, w_ada, b_ada):
    bsz, d = c.shape
    n = w_ada.shape[1]
    tn = 1024
    return pl.pallas_call(
        _ada_kernel,
        grid=(n // tn,),
        in_specs=[pl.BlockSpec((bsz, d), lambda j: (0, 0)),
                  pl.BlockSpec((d, tn), lambda j: (0, j)),
                  pl.BlockSpec((1, tn), lambda j: (0, j))],
        out_specs=pl.BlockSpec((bsz, tn), lambda j: (0, j)),
        out_shape=jax.ShapeDtypeStruct((bsz, n), F32),
        compiler_params=_cparams(("parallel",)),
        name="ada_mod",
    )(c, w_ada, b_ada.reshape(1, n))


def _inproj_kernel(x_ref, g_ref, sc_ref, sh_ref, w_ref, b_ref, o_ref, h_ref):
    @pl.when(pl.program_id(2) == 0)
    def _():
        h_ref[...] = _rms_mod(x_ref[...], g_ref[...], sc_ref[...], sh_ref[...]).astype(BF16)

    o_ref[...] = jnp.dot(h_ref[...], w_ref[...], preferred_element_type=F32) + b_ref[...]


def _in_proj(x, g_mix, scale1, shift1, w_in_p, b_in_p):
    bsz, s, d = x.shape
    n = w_in_p.shape[1]
    tm = min(1024, s)
    tn = 512
    return pl.pallas_call(
        _inproj_kernel,
        grid=(bsz, s // tm, n // tn),
        in_specs=[pl.BlockSpec((None, tm, d), lambda b, i, j: (b, i, 0)),
                  pl.BlockSpec((1, d), lambda b, i, j: (0, 0)),
                  pl.BlockSpec((None, 1, d), lambda b, i, j: (b, 0, 0)),
                  pl.BlockSpec((None, 1, d), lambda b, i, j: (b, 0, 0)),
                  pl.BlockSpec((d, tn), lambda b, i, j: (0, j)),
                  pl.BlockSpec((1, tn), lambda b, i, j: (0, j))],
        out_specs=pl.BlockSpec((None, tm, tn), lambda b, i, j: (b, i, j)),
        out_shape=jax.ShapeDtypeStruct((bsz, s, n), F32),
        scratch_shapes=[pltpu.VMEM((tm, d), BF16)],
        compiler_params=_cparams(("parallel", "parallel", "arbitrary")),
        name="in_proj",
    )(x, g_mix.reshape(1, d), scale1, shift1, w_in_p, b_in_p.reshape(1, n))


def _rope_kernel(pos_ref, freq_ref, c_ref, s1_ref, s2_ref):
    ang = pos_ref[...] * freq_ref[...]
    cs = jnp.cos(ang)
    sn = jnp.sin(ang)
    d = lax.broadcasted_iota(jnp.int32, ang.shape, 1) & (HEAD_DIM - 1)
    c_ref[...] = cs
    s1_ref[...] = jnp.where(d < ROPE_DIMS // 2, -sn, 0.0)
    s2_ref[...] = jnp.where((d >= ROPE_DIMS // 2) & (d < ROPE_DIMS), sn, 0.0)


def _rope_tables(positions):
    bsz, s = positions.shape
    t = bsz * s
    tm = min(1024, t)
    inv_freq = ROPE_THETA ** (-(jnp.arange(0, ROPE_DIMS, 2, dtype=F32) / ROPE_DIMS))
    d = jnp.arange(LANES) % HEAD_DIM
    freq = jnp.where(d < ROPE_DIMS, inv_freq[d % (ROPE_DIMS // 2)], 0.0).astype(F32).reshape(1, LANES)
    pos = positions.astype(F32).reshape(t, 1)
    sds = jax.ShapeDtypeStruct((t, LANES), F32)
    outs = pl.pallas_call(
        _rope_kernel,
        grid=(t // tm,),
        in_specs=[pl.BlockSpec((tm, 1), lambda i: (i, 0)),
                  pl.BlockSpec((1, LANES), lambda i: (0, 0))],
        out_specs=[pl.BlockSpec((tm, LANES), lambda i: (i, 0))] * 3,
        out_shape=[sds, sds, sds],
        compiler_params=_cparams(("parallel",)),
        name="rope_tables",
    )(pos, freq)
    return [o.reshape(bsz, s, LANES) for o in outs]


def _rope_apply(t, c, s1, s2):
    outs = []
    for j in range(t.shape[1] // LANES):
        tj = t[:, j * LANES:(j + 1) * LANES]
        outs.append(tj * c + pltpu.roll(tj, LANES - ROPE_DIMS // 2, 1) * s1
                    + pltpu.roll(tj, ROPE_DIMS // 2, 1) * s2)
    return jnp.concatenate(outs, axis=1)


def _attn_kernel(sink_ref, q_ref, kvc_ref, kvp_ref, cq_ref, s1q_ref, s2q_ref,
                 cp_ref, s1p_ref, s2p_ref, bm_ref, o_ref):
    n = pl.program_id(1)
    cq, s1q, s2q = cq_ref[...], s1q_ref[...], s2q_ref[...]
    q = (_rope_apply(q_ref[...], cq, s1q, s2q) * (HEAD_DIM ** -0.5)).astype(BF16)
    kvc = kvc_ref[...]
    kvp = kvp_ref[...]
    kc = _rope_apply(kvc[:, :KV_WIDTH], cq, s1q, s2q)
    kp = _rope_apply(kvp[:, :KV_WIDTH], cp_ref[...], s1p_ref[...], s2p_ref[...])
    kband = jnp.concatenate([kp, kc], axis=0).astype(BF16)
    vband = jnp.concatenate([kvp[:, KV_WIDTH:], kvc[:, KV_WIDTH:]], axis=0).astype(BF16)
    bm = bm_ref[...]
    kexp = jnp.concatenate([kband] * N_KV_HEADS, axis=0) * bm
    vexp = jnp.concatenate([vband] * N_KV_HEADS, axis=0) * bm
    row = lax.broadcasted_iota(jnp.int32, (WINDOW, 2 * WINDOW), 0)
    col = lax.broadcasted_iota(jnp.int32, (WINDOW, 2 * WINDOW), 1)
    diff = row + WINDOW - col
    valid = (diff >= 0) & (diff < WINDOW) & ((col >= WINDOW) | (n > 0))
    rep = N_Q_HEADS // N_KV_HEADS
    kw = 2 * WINDOW
    for r in range(rep):
        qr = q[:, r * KV_WIDTH:(r + 1) * KV_WIDTH]
        sc = lax.dot_general(qr, kexp, (((1,), (1,)), ((), ())), preferred_element_type=F32)
        ps = []
        for g in range(N_KV_HEADS):
            s = jnp.where(valid, sc[:, g * kw:(g + 1) * kw], MASK_VALUE)
            sink = sink_ref[g * rep + r]
            m = jnp.maximum(jnp.max(s, axis=-1, keepdims=True), sink)
            p = jnp.exp(s - m)
            l = jnp.sum(p, axis=-1, keepdims=True) + jnp.exp(sink - m)
            ps.append((p / l).astype(BF16))
        pcat = jnp.concatenate(ps, axis=1)
        o = jnp.dot(pcat, vexp, preferred_element_type=F32)
        o_ref[:, r * KV_WIDTH:(r + 1) * KV_WIDTH] = o.astype(BF16)


def _attention(proj, sinks, ctab, s1tab, s2tab):
    bsz, s, _ = proj.shape
    nb = s // WINDOW
    g_row = jnp.arange(N_KV_HEADS * 2 * WINDOW) // (2 * WINDOW)
    g_col = jnp.arange(KV_WIDTH) // HEAD_DIM
    bm = (g_row[:, None] == g_col[None, :]).astype(BF16)
    cur = lambda b, n: (b, n, 0)
    prev = lambda b, n: (b, jnp.maximum(n - 1, 0), 0)
    kvw = 2 * KV_WIDTH
    tab = pl.BlockSpec((None, WINDOW, LANES), cur)
    tabp = pl.BlockSpec((None, WINDOW, LANES), prev)
    return pl.pallas_call(
        _attn_kernel,
        grid=(bsz, nb),
        in_specs=[pl.BlockSpec(memory_space=pltpu.SMEM),
                  pl.BlockSpec((None, WINDOW, ATTN_WIDTH), cur),
                  pl.BlockSpec((None, WINDOW, kvw), lambda b, n: (b, n, COL_KV // kvw)),
                  pl.BlockSpec((None, WINDOW, kvw), lambda b, n: (b, jnp.maximum(n - 1, 0), COL_KV // kvw)),
                  tab, tab, tab, tabp, tabp, tabp,
                  pl.BlockSpec((N_KV_HEADS * 2 * WINDOW, KV_WIDTH), lambda b, n: (0, 0))],
        out_specs=pl.BlockSpec((None, WINDOW, ATTN_WIDTH), cur),
        out_shape=jax.ShapeDtypeStruct((bsz, s, ATTN_WIDTH), BF16),
        compiler_params=_cparams(("parallel", "parallel")),
        name="attention",
    )(sinks, proj, proj, proj, ctab, s1tab, s2tab, ctab, s1tab, s2tab, bm)


def _ssm_prep_kernel(are_ref, aim_ref, ldt_ref, bre_ref, bim_ref,
                     abre_ref, abim_ref, bbre_ref, bbim_ref):
    dt = jnp.exp(ldt_ref[...])
    lam_re = jnp.minimum(are_ref[...], -1e-4)
    lam_im = aim_ref[...]
    mag = jnp.exp(lam_re * dt)
    ab_re = mag * jnp.cos(lam_im * dt)
    ab_im = mag * jnp.sin(lam_im * dt)
    den = lam_re * lam_re + lam_im * lam_im
    num_re = ab_re - 1.0
    z_re = (num_re * lam_re + ab_im * lam_im) / den
    z_im = (ab_im * lam_re - num_re * lam_im) / den
    abre_ref[...] = ab_re
    abim_ref[...] = ab_im
    bre = bre_ref[...]
    bim = bim_ref[...]
    bbre_ref[...] = z_re * bre - z_im * bim
    bbim_ref[...] = z_re * bim + z_im * bre


def _ssm_prep(a_re, a_im, log_dt, b_re, b_im):
    g, p, h = b_re.shape
    a3 = jax.ShapeDtypeStruct((g, 1, p), F32)
    b3 = jax.ShapeDtypeStruct((g, h, p), F32)
    return pl.pallas_call(
        _ssm_prep_kernel,
        out_shape=[a3, a3, b3, b3],
        name="ssm_prep",
    )(a_re.reshape(g, 1, p), a_im.reshape(g, 1, p), log_dt.reshape(g, 1, 1),
      jnp.swapaxes(b_re, 1, 2), jnp.swapaxes(b_im, 1, 2))


def _ssm_kernel(u_ref, wb_ref, wc_ref, are_ref, aim_ref, d_ref, wg_ref, bg_ref, o_ref,
                xre, xim, sre, sim, *, n_batch, t_chunk, lane_chunk):
    @pl.when(pl.program_id(0) == 0)
    def _():
        sre[...] = jnp.zeros_like(sre)
        sim[...] = jnp.zeros_like(sim)

    n_slices = SSM_WIDTH // LANES
    sw = SSM_SLICE_GROUPS * SSM_STATE
    u = u_ref[...]
    ub = u.astype(BF16)
    for s in range(n_slices):
        bu = jnp.dot(ub[:, s * LANES:(s + 1) * LANES], wb_ref[s], preferred_element_type=F32)
        xre[:, s * sw:(s + 1) * sw] = bu[:, :sw]
        xim[:, s * sw:(s + 1) * sw] = bu[:, sw:]

    for lc in range(STATE_LANES // lane_chunk):
        sl = slice(lc * lane_chunk, (lc + 1) * lane_chunk)
        ar = jnp.broadcast_to(are_ref[:, sl], (n_batch, lane_chunk))
        ai = jnp.broadcast_to(aim_ref[:, sl], (n_batch, lane_chunk))

        def body(t, carry, sl=sl, ar=ar, ai=ai):
            cr, ci = carry
            off = pl.multiple_of(t * n_batch, n_batch)
            nr = ar * cr - ai * ci + xre[pl.ds(off, n_batch), sl]
            ni = ar * ci + ai * cr + xim[pl.ds(off, n_batch), sl]
            xre[pl.ds(off, n_batch), sl] = nr
            xim[pl.ds(off, n_batch), sl] = ni
            return nr, ni

        cr, ci = lax.fori_loop(0, t_chunk, body, (sre[:, sl], sim[:, sl]), unroll=8)
        sre[:, sl] = cr
        sim[:, sl] = ci

    ys = []
    for s in range(n_slices):
        xs = jnp.concatenate([xre[:, s * sw:(s + 1) * sw], xim[:, s * sw:(s + 1) * sw]], axis=1)
        ys.append(jnp.dot(xs.astype(BF16), wc_ref[s], preferred_element_type=F32))
    y = jnp.concatenate(ys, axis=1) + d_ref[...] * u
    z = jax.nn.gelu(y)
    gl = jnp.dot(z.astype(BF16), wg_ref[...], preferred_element_type=F32) + bg_ref[...]
    o_ref[...] = (z * jax.nn.sigmoid(gl)).astype(BF16)


def _ssm(u_tm, n_batch, wb, wc, ab_re, ab_im, d_skip, w_glu, b_glu):
    rows, w = u_tm.shape
    s = rows // n_batch
    t_chunk = min(64, s)
    r = t_chunk * n_batch
    full = lambda shape: pl.BlockSpec(shape, lambda i: (0,) * len(shape))
    kern = functools.partial(_ssm_kernel, n_batch=n_batch, t_chunk=t_chunk, lane_chunk=512)
    return pl.pallas_call(
        kern,
        grid=(s // t_chunk,),
        in_specs=[pl.BlockSpec((r, w), lambda i: (i, 0)),
                  full(wb.shape), full(wc.shape), full((1, STATE_LANES)), full((1, STATE_LANES)),
                  full((1, w)), full((w, w)), full((1, w))],
        out_specs=pl.BlockSpec((r, w), lambda i: (i, 0)),
        out_shape=jax.ShapeDtypeStruct((rows, w), BF16),
        scratch_shapes=[pltpu.VMEM((r, STATE_LANES), F32), pltpu.VMEM((r, STATE_LANES), F32),
                        pltpu.VMEM((n_batch, STATE_LANES), F32), pltpu.VMEM((n_batch, STATE_LANES), F32)],
        compiler_params=_cparams(("arbitrary",)),
        name="ssm",
    )(u_tm, wb, wc, ab_re, ab_im, d_skip.reshape(1, w), w_glu, b_glu.reshape(1, w))


def _merge_kernel(attn_ref, sso_ref, ga_ref, gs_ref, x_ref, g1_ref, gf_ref, sc2_ref, sh2_ref,
                  wab_ref, wsb_ref, wo_ref, x1_ref, hi_ref, lo_ref):
    ya = jnp.dot(attn_ref[...], wab_ref[...], preferred_element_type=F32)
    ys = jnp.dot(sso_ref[...], wsb_ref[...], preferred_element_type=F32)
    merged = jax.nn.sigmoid(ga_ref[...]) * ya + jax.nn.sigmoid(gs_ref[...]) * ys
    mo = jnp.dot(merged.astype(BF16), wo_ref[...], preferred_element_type=F32)
    x1 = x_ref[...] + g1_ref[...] * mo
    x1_ref[...] = x1
    h2t = _rms_mod(x1, gf_ref[...], sc2_ref[...], sh2_ref[...]).T
    hi = h2t.astype(BF16)
    hi_ref[...] = hi
    lo_ref[...] = (h2t - hi.astype(F32)).astype(BF16)


def _merge(attn, sso, proj, x, gate1, g_ffn, scale2, shift2, wab, wsb, wo):
    bsz, s, d = x.shape
    tm = min(256, s)
    nt = s // tm
    row = lambda w, cb: pl.BlockSpec((None, tm, w), lambda b, i: (b, i, cb))
    per_b = pl.BlockSpec((None, 1, d), lambda b, i: (b, 0, 0))
    const = lambda shape: pl.BlockSpec(shape, lambda b, i: (0, 0), pipeline_mode=pl.Buffered(1))
    tsds = jax.ShapeDtypeStruct((d, bsz * s), BF16)
    tspec = pl.BlockSpec((d, tm), lambda b, i: (0, b * nt + i))
    return pl.pallas_call(
        _merge_kernel,
        grid=(bsz, nt),
        in_specs=[row(ATTN_WIDTH, 0), row(SSM_WIDTH, 0), row(d, COL_GA // d), row(d, COL_GS // d),
                  row(d, 0), per_b, pl.BlockSpec((1, d), lambda b, i: (0, 0)), per_b, per_b,
                  const(wab.shape), const(wsb.shape), const(wo.shape)],
        out_specs=[row(d, 0), tspec, tspec],
        out_shape=[jax.ShapeDtypeStruct((bsz, s, d), F32), tsds, tsds],
        compiler_params=_cparams(("parallel", "parallel")),
        name="merge",
    )(attn, sso, proj, proj, x, gate1, g_ffn.reshape(1, d), scale2, shift2, wab, wsb, wo)


def _dot3(a_hi, a_lo, b_hi, b_lo):
    return (jnp.dot(a_hi, b_hi, preferred_element_type=F32)
            + jnp.dot(a_hi, b_lo, preferred_element_type=F32)
            + jnp.dot(a_lo, b_hi, preferred_element_type=F32))


def _top16(v):
    idx = lax.broadcasted_iota(jnp.int32, v.shape, 0)
    rank = jnp.full(v.shape, 100.0, F32)
    vals = []
    for it in range(PEER_TOPK):
        m = jnp.max(v, axis=0, keepdims=True)
        istar = jnp.min(jnp.where(v == m, idx, PEER_NKEYS), axis=0, keepdims=True)
        sel = idx == istar
        rank = jnp.where(sel, float(it + 1), rank)
        v = jnp.where(sel, -jnp.inf, v)
        vals.append(m)
    return jnp.concatenate(vals, axis=0), rank


def _staircase(a, b):
    lanes = a.shape[1]
    rowi = lax.broadcasted_iota(jnp.int32, (8, lanes), 0)
    cs, codes = [], []
    for q in range(PEER_TOPK):
        nq = PEER_TOPK // (q + 1)
        for blk in range((nq + 7) // 8):
            p = rowi + 8 * blk
            c = a[8 * blk:8 * blk + 8, :] + b[q:q + 1, :]
            cs.append(jnp.where(p < nq, c, -jnp.inf))
            codes.append(p * PEER_TOPK + q)
    cand = jnp.concatenate(cs, axis=0)
    code = jnp.concatenate(codes, axis=0)
    row16 = lax.broadcasted_iota(jnp.int32, (PEER_TOPK, lanes), 0)
    cnt = jnp.zeros((PEER_TOPK, lanes), F32)
    m0 = a[0:1, :] + b[0:1, :]
    z = jnp.zeros((1, lanes), F32)
    for _ in range(PEER_TOPK):
        m = jnp.max(cand, axis=0, keepdims=True)
        cstar = jnp.min(jnp.where(cand == m, code, 1 << 20), axis=0, keepdims=True)
        cand = jnp.where(code == cstar, -jnp.inf, cand)
        z = z + jnp.exp(m - m0)
        cnt = cnt + jnp.where(row16 == (cstar >> 4), 1.0, 0.0)
    return cnt, z


def _batcher_pairs(n):
    pairs = []

    def merge(lo, hi, r):
        step = r * 2
        if step < hi - lo:
            merge(lo, hi, step)
            merge(lo + r, hi, step)
            pairs.extend((i, i + r) for i in range(lo + r, hi - r, step))
        else:
            pairs.append((lo, lo + r))

    def sort(lo, hi):
        if hi - lo >= 1:
            mid = lo + (hi - lo) // 2
            sort(lo, mid)
            sort(mid + 1, hi)
            merge(lo, hi, 1)

    sort(0, n - 1)
    return pairs


_SORT16 = _batcher_pairs(PEER_TOPK)
SUBLANES = 8


def _cmpx(v, i, j):
    hi, lo = jnp.maximum(v[i], v[j]), jnp.minimum(v[i], v[j])
    v[i], v[j] = hi, lo


def _sub_allreduce(x, op):
    for shift in (4, 2, 1):
        x = op(x, pltpu.roll(x, shift, 0))
    return x


def _sorted_top16(s):
    v = [s[SUBLANES * k:SUBLANES * (k + 1), :] for k in range(PEER_TOPK)]
    for i, j in _SORT16:
        _cmpx(v, i, j)
    for shift in (4, 2, 1):
        m = [jnp.maximum(v[k], pltpu.roll(v[PEER_TOPK - 1 - k], shift, 0)) for k in range(PEER_TOPK)]
        for d in (8, 4, 2, 1):
            for i in range(PEER_TOPK):
                if not i & d:
                    _cmpx(m, i, i + d)
        v = m
    return v


def _count_sorted(b, passes):
    t3 = passes(b[7])
    t2 = passes(jnp.where(t3, b[11], b[3]))
    t1 = passes(jnp.where(t3, jnp.where(t2, b[13], b[9]), jnp.where(t2, b[5], b[1])))
    c0 = jnp.where(t3,
                   jnp.where(t2, jnp.where(t1, b[14], b[12]), jnp.where(t1, b[10], b[8])),
                   jnp.where(t2, jnp.where(t1, b[6], b[4]), jnp.where(t1, b[2], b[0])))
    t0 = passes(c0)
    cnt = (jnp.where(t3, 8.0, 0.0) + jnp.where(t2, 4.0, 0.0)
           + jnp.where(t1, 2.0, 0.0) + jnp.where(t0, 1.0, 0.0))
    return jnp.where(passes(b[15]), 16.0, cnt)


def _route_fast(s0, s1):
    a = _sorted_top16(s0)
    b = _sorted_top16(s1)
    sub = lax.broadcasted_iota(jnp.int32, a[0].shape, 0)
    ninf = -jnp.inf

    def pick(vs, base):
        acc = vs[base + SUBLANES - 1]
        for r in range(SUBLANES - 2, -1, -1):
            acc = jnp.where(sub == r, vs[base + r], acc)
        return acc

    a_lo, a_hi, b_lo, b_hi = pick(a, 0), pick(a, 8), pick(b, 0), pick(b, 8)
    ge2 = sub >= 2
    packs = [a[0] + b_lo, a[0] + b_hi, a[1] + b_lo, a_hi + b[0],
             jnp.where(ge2, a_lo + b[0], ninf), jnp.where(ge2, a_lo + b[1], ninf),
             jnp.where(ge2 & (sub <= 4), a_lo + b[2], ninf),
             jnp.where((sub == 3) | (sub == 4), a[2] + b_lo, jnp.where(sub == 5, a[3] + b[3], ninf))]
    work = list(packs)
    tau = None
    for it in range(PEER_TOPK):
        m = work[0]
        for w in work[1:]:
            m = jnp.maximum(m, w)
        tau = _sub_allreduce(m, jnp.maximum)
        if it < PEER_TOPK - 1:
            work = [jnp.where(w == tau, ninf, w) for w in work]
    m0 = a[0] + b[0]
    z = None
    for p in packs:
        term = jnp.where(p >= tau, jnp.exp(p - m0), 0.0)
        z = term if z is None else z + term
    z = _sub_allreduce(z, jnp.add)

    g, l, e1, r1 = [], [], [], []
    lsum = c16 = rsum = None
    for k in range(PEER_NKEYS // SUBLANES):
        s0v = s0[SUBLANES * k:SUBLANES * (k + 1), :]
        s1v = s1[SUBLANES * k:SUBLANES * (k + 1), :]
        lv = _count_sorted(b, lambda x, s0v=s0v: (s0v + x) >= tau)
        rv = _count_sorted(b, lambda x, s1v=s1v: x > s1v) + 1.0
        in16 = rv <= float(PEER_TOPK)
        one = jnp.where(in16, 1.0, 0.0)
        rin = jnp.where(in16, rv, 0.0)
        lsum = lv if lsum is None else lsum + lv
        c16 = one if c16 is None else c16 + one
        rsum = rin if rsum is None else rsum + rin
        g.append(jnp.exp(s0v - a[0]) / z)
        l.append(lv)
        e1.append(jnp.exp(s1v - b[0]))
        r1.append(rv)
    lsum = _sub_allreduce(lsum, jnp.add)
    c16 = _sub_allreduce(c16, jnp.add)
    rsum = _sub_allreduce(rsum, jnp.add)
    n = float(PEER_TOPK)
    clean = (lsum == n) & (c16 == n) & (rsum == n * (n + 1.0) / 2.0)
    return g, l, e1, r1, jnp.where(clean, 0.0, 1.0)


def _route_exact(s0, s1):
    a, rank0 = _top16(s0)
    b, rank1 = _top16(s1)
    cnt, z = _staircase(a, b)
    lsel = jnp.zeros(s0.shape, F32)
    for p in range(PEER_TOPK):
        lsel = lsel + jnp.where(rank0 == float(p + 1), cnt[p:p + 1, :], 0.0)
    return jnp.exp(s0 - a[0:1, :]) / z, lsel, jnp.exp(s1 - b[0:1, :]), rank1


def _kw_kernel(kh_ref, kl_ref, wh_ref, wl_ref, o_ref):
    o_ref[...] = _dot3(kh_ref[...], kl_ref[...], wh_ref[...], wl_ref[...])


def _key_query_weights(k_hi, k_lo, wq_hi, wq_lo):
    n, nk, hd = k_hi.shape
    d = wq_hi.shape[-1]
    ks = pl.BlockSpec((None, nk, hd), lambda i: (i, 0, 0))
    ws = pl.BlockSpec((None, hd, d), lambda i: (i, 0, 0))
    return pl.pallas_call(
        _kw_kernel,
        grid=(n,),
        in_specs=[ks, ks, ws, ws],
        out_specs=pl.BlockSpec((None, nk, d), lambda i: (i, 0, 0)),
        out_shape=jax.ShapeDtypeStruct((n, nk, d), F32),
        compiler_params=_cparams(("parallel",)),
        name="peer_kw",
    )(k_hi, k_lo, wq_hi, wq_lo)


def _route_kernel(xh_ref, xl_ref, wh_ref, wl_ref, g_ref, l_ref, e1_ref, r1_ref, sc_ref):
    sc_ref[...] = _dot3(wh_ref[...], wl_ref[...], xh_ref[...], xl_ref[...])
    ng = sc_ref.shape[1] // LANES

    def lane_group(idx, carry):
        h = idx // ng
        off = pl.multiple_of((idx % ng) * LANES, LANES)
        row = pl.multiple_of(h * (2 * PEER_NKEYS), 2 * PEER_NKEYS)
        s0 = sc_ref[pl.ds(row, PEER_NKEYS), pl.ds(off, LANES)]
        s1 = sc_ref[pl.ds(row + PEER_NKEYS, PEER_NKEYS), pl.ds(off, LANES)]
        g, l, e1, r1, bad = _route_fast(s0, s1)
        cat = lambda pieces: jnp.concatenate(pieces, axis=0)
        g_ref[h, :, pl.ds(off, LANES)] = cat(g)
        l_ref[h, :, pl.ds(off, LANES)] = cat(l)
        e1_ref[h, :, pl.ds(off, LANES)] = cat(e1).astype(BF16)
        r1_ref[h, :, pl.ds(off, LANES)] = cat(r1).astype(BF16)

        @pl.when(jnp.max(bad) > 0.0)
        def _():
            gx, lx, e1x, r1x = _route_exact(s0, s1)
            g_ref[h, :, pl.ds(off, LANES)] = gx
            l_ref[h, :, pl.ds(off, LANES)] = lx
            e1_ref[h, :, pl.ds(off, LANES)] = e1x.astype(BF16)
            r1_ref[h, :, pl.ds(off, LANES)] = r1x.astype(BF16)

        return carry

    lax.fori_loop(0, PEER_HEADS * ng, lane_group, 0)


def _peer_route(h2t_hi, h2t_lo, kw_hi, kw_lo):
    d, t = h2t_hi.shape
    tn = min(256, t)
    nq = kw_hi.shape[0]
    xs = pl.BlockSpec((d, tn), lambda i: (0, i))
    ws = pl.BlockSpec((nq, d), lambda i: (0, 0), pipeline_mode=pl.Buffered(1))
    os_ = pl.BlockSpec((PEER_HEADS, PEER_NKEYS, tn), lambda i: (0, 0, i))
    f = jax.ShapeDtypeStruct((PEER_HEADS, PEER_NKEYS, t), F32)
    bf = jax.ShapeDtypeStruct((PEER_HEADS, PEER_NKEYS, t), BF16)
    return pl.pallas_call(
        _route_kernel,
        grid=(t // tn,),
        in_specs=[xs, xs, ws, ws],
        out_specs=[os_, os_, os_, os_],
        out_shape=[f, f, bf, bf],
        scratch_shapes=[pltpu.VMEM((nq, tn), F32)],
        compiler_params=_cparams(("parallel",)),
        name="peer_route",
    )(h2t_hi, h2t_lo, kw_hi, kw_lo)


def _dense_kernel(h_ref, dn_ref, up_ref, g_ref, l_ref, e1_ref, r1_ref, o_ref, *, n_i):
    @pl.when(pl.program_id(1) == 0)
    def _():
        o_ref[...] = jnp.zeros_like(o_ref)

    hx = h_ref[...]
    wgs = []
    ipc = 1
    for ic in range(n_i // ipc):
        rows = slice(ic * ipc * PEER_NKEYS, (ic + 1) * ipc * PEER_NKEYS)
        at = jnp.dot(dn_ref[rows, :], hx, preferred_element_type=F32)
        ws = []
        for il in range(ic * ipc, (ic + 1) * ipc):
            w = None
            for h in range(PEER_HEADS):
                gi = g_ref[h, il:il + 1, :].astype(BF16)
                li = l_ref[h, il:il + 1, :].astype(BF16)
                term = jnp.where(r1_ref[h] <= li, e1_ref[h] * gi, jnp.zeros((), BF16))
                w = term if w is None else w + term
            ws.append(w)
        wgs.append(jax.nn.gelu(at).astype(BF16) * jnp.concatenate(ws, axis=0))
    wg = jnp.concatenate(wgs, axis=0)
    o_ref[...] += jnp.dot(up_ref[...], wg, preferred_element_type=F32)


def _peer_dense(h2t_hi, down_bf, up_t_bf, g, l, e1, r1):
    d, t = h2t_hi.shape
    ne = down_bf.shape[0]
    tm = min(512, t)
    en = 1024
    n_i = en // PEER_NKEYS
    rt = lambda n: pl.BlockSpec((PEER_HEADS, n, tm), lambda i, e: (0, e if n == n_i else 0, i))
    return pl.pallas_call(
        functools.partial(_dense_kernel, n_i=n_i),
        grid=(t // tm, ne // en),
        in_specs=[pl.BlockSpec((d, tm), lambda i, e: (0, i)),
                  pl.BlockSpec((en, d), lambda i, e: (e, 0)),
                  pl.BlockSpec((d, en), lambda i, e: (0, e)),
                  rt(n_i), rt(n_i), rt(PEER_NKEYS), rt(PEER_NKEYS)],
        out_specs=pl.BlockSpec((d, tm), lambda i, e: (0, i)),
        out_shape=jax.ShapeDtypeStruct((d, t), F32),
        compiler_params=_cparams(("parallel", "arbitrary")),
        name="peer_dense",
    )(h2t_hi, down_bf, up_t_bf, g, l, e1, r1)


def _final_kernel(x1_ref, pt_ref, g2_ref, gf_ref, o_ref):
    x2 = x1_ref[...] + g2_ref[...] * pt_ref[...].T
    ms = jnp.mean(x2 * x2, axis=-1, keepdims=True)
    o_ref[...] = x2 * lax.rsqrt(ms + RMS_EPS) * gf_ref[...]


def _final(x1, peer_t, gate2, g_final):
    bsz, s, d = x1.shape
    tm = min(512, s)
    nt = s // tm
    row = pl.BlockSpec((None, tm, d), lambda b, i: (b, i, 0))
    return pl.pallas_call(
        _final_kernel,
        grid=(bsz, nt),
        in_specs=[row, pl.BlockSpec((d, tm), lambda b, i: (0, b * nt + i)),
                  pl.BlockSpec((None, 1, d), lambda b, i: (b, 0, 0)),
                  pl.BlockSpec((1, d), lambda b, i: (0, 0))],
        out_specs=row,
        out_shape=jax.ShapeDtypeStruct((bsz, s, d), F32),
        compiler_params=_cparams(("parallel", "parallel")),
        name="final",
    )(x1, peer_t, gate2, g_final.reshape(1, d))


def _head_perm(w, axis):
    rep = N_Q_HEADS // N_KV_HEADS
    shape = w.shape
    w = w.reshape(shape[:axis] + (N_KV_HEADS, rep, HEAD_DIM) + shape[axis + 1:])
    return jnp.swapaxes(w, axis, axis + 1).reshape(shape)


def _split_bf16(w):
    hi = w.astype(BF16)
    return hi, (w - hi.astype(F32)).astype(BF16)


def _layer(x, c, positions, w_ada, b_ada, g_mix, w_in, b_in, attn_sinks, w_attn_branch,
           ssm_A_re, ssm_A_im, ssm_log_dt, ssm_B_re, ssm_B_im, ssm_C_re, ssm_C_im, ssm_D,
           w_glu, b_glu, w_ssm_branch, w_out, g_ffn, w_query, sub_keys, expert_down, expert_up):
    bsz, s, d = x.shape
    mod = _ada_mod(c, w_ada, b_ada)
    shift1, scale1, gate1, shift2, scale2, gate2 = [m.reshape(bsz, 1, d) for m in jnp.split(mod, 6, axis=-1)]

    o1, o3, o4 = ATTN_WIDTH, ATTN_WIDTH + 2 * KV_WIDTH, ATTN_WIDTH + 2 * KV_WIDTH + SSM_WIDTH
    permute = lambda w: jnp.concatenate(
        [_head_perm(w[..., :o1], w.ndim - 1), w[..., o3:o4], w[..., o4:], w[..., o1:o3]], axis=-1)
    proj = _in_proj(x, g_mix, scale1, shift1, permute(w_in).astype(BF16), permute(b_in))

    ctab, s1tab, s2tab = _rope_tables(positions)
    attn = _attention(proj, attn_sinks, ctab, s1tab, s2tab)

    ab_re, ab_im, bb_re, bb_im = _ssm_prep(ssm_A_re, ssm_A_im, ssm_log_dt, ssm_B_re, ssm_B_im)
    ns = N_SSM_GROUPS // SSM_SLICE_GROUPS
    eye = jnp.eye(SSM_SLICE_GROUPS, dtype=F32)
    shp = (ns, SSM_SLICE_GROUPS, SSM_GROUP, SSM_STATE)
    blk_b = lambda m: jnp.einsum("sghp,gk->sghkp", m.reshape(shp), eye).reshape(ns, LANES, -1)
    blk_c = lambda m: jnp.einsum("sghp,gk->sgpkh", m.reshape(shp), eye).reshape(ns, -1, LANES)
    wb = jnp.concatenate([blk_b(bb_re), blk_b(bb_im)], axis=-1).astype(BF16)
    wc = jnp.concatenate([blk_c(ssm_C_re), -blk_c(ssm_C_im)], axis=1).astype(BF16)
    u_tm = jnp.swapaxes(proj[:, :, COL_U:COL_U + SSM_WIDTH], 0, 1).reshape(s * bsz, SSM_WIDTH)
    sso_tm = _ssm(u_tm, bsz, wb, wc, ab_re.reshape(1, -1), ab_im.reshape(1, -1), ssm_D,
                  w_glu.astype(BF16), b_glu)
    sso = jnp.swapaxes(sso_tm.reshape(s, bsz, SSM_WIDTH), 0, 1)

    x1, h2t_hi, h2t_lo = _merge(attn, sso, proj, x, gate1, g_ffn, scale2, shift2,
                                _head_perm(w_attn_branch, 0).astype(BF16), w_ssm_branch.astype(BF16),
                                w_out.astype(BF16))

    wq_hi, wq_lo = _split_bf16(w_query.T.reshape(2 * PEER_HEADS, PEER_HALF, d))
    k_hi, k_lo = _split_bf16(sub_keys.reshape(2 * PEER_HEADS, PEER_NKEYS, PEER_HALF))
    kw_hi, kw_lo = _split_bf16(_key_query_weights(k_hi, k_lo, wq_hi, wq_lo).reshape(-1, d))
    g, l, e1, r1 = _peer_route(h2t_hi, h2t_lo, kw_hi, kw_lo)
    peer_t = _peer_dense(h2t_hi, expert_down.astype(BF16), expert_up.astype(BF16).T, g, l, e1, r1)
    return x1, peer_t, gate2


def kernel(x, c, positions, w_ada, b_ada, g_mix, w_in, b_in, attn_sinks, w_attn_branch, ssm_A_re, ssm_A_im, ssm_log_dt, ssm_B_re, ssm_B_im, ssm_C_re, ssm_C_im, ssm_D, w_glu, b_glu, w_ssm_branch, w_out, g_ffn, w_query, sub_keys, expert_down, expert_up, g_final):
    depth = w_ada.shape[0]
    assert depth == 1
    x1, peer_t, gate2 = _layer(x, c, positions, w_ada[0], b_ada[0], g_mix[0], w_in[0], b_in[0],
                               attn_sinks[0], w_attn_branch[0], ssm_A_re[0], ssm_A_im[0], ssm_log_dt[0],
                               ssm_B_re[0], ssm_B_im[0], ssm_C_re[0], ssm_C_im[0], ssm_D[0], w_glu[0],
                               b_glu[0], w_ssm_branch[0], w_out[0], g_ffn[0], w_query[0], sub_keys[0],
                               expert_down[0], expert_up[0])
    return _final(x1, peer_t, gate2, g_final)
```

```python
import functools
import math

import jax
import jax.numpy as jnp
from jax import lax
from jax.experimental import pallas as pl
from jax.experimental.pallas import tpu as pltpu

F32 = jnp.float32
BF16 = jnp.bfloat16

D_MODEL = 2048
ATTN_WIDTH = 1024
SSM_WIDTH = 1024
HEAD_DIM = 64
N_Q_HEADS = 16
N_KV_HEADS = 4
KV_WIDTH = 256
WINDOW = 128
ROPE_THETA = 500000.0
ROPE_DIMS = 16
SSM_GROUP = 16
N_SSM_GROUPS = 64
SSM_STATE = 64
STATE_LANES = N_SSM_GROUPS * SSM_STATE
PEER_HEADS = 8
PEER_NKEYS = 128
PEER_N = PEER_NKEYS * PEER_NKEYS
PEER_HALF = 128
PEER_TOPK = 16
RMS_EPS = 1e-6
MASK_VALUE = -1e30
IN_COLS = ATTN_WIDTH + 2 * KV_WIDTH + SSM_WIDTH + 2 * D_MODEL

LANES = 128
SSM_SLICE_GROUPS = 8
VMEM_LIMIT = 56 * 1024 * 1024

COL_Q = 0
COL_U = ATTN_WIDTH
COL_GA = COL_U + SSM_WIDTH
COL_GS = COL_GA + D_MODEL
COL_KV = COL_GS + D_MODEL


def _cparams(sem):
    return pltpu.CompilerParams(dimension_semantics=sem, vmem_limit_bytes=VMEM_LIMIT)


def _rms_mod(x, g, scale, shift):
    ms = jnp.mean(x * x, axis=-1, keepdims=True)
    y = x * lax.rsqrt(ms + RMS_EPS) * g
    return y * (1.0 + scale) + shift


def _ada_kernel(c_ref, w_ref, b_ref, o_ref):
    c = c_ref[...]
    sc = c * jax.nn.sigmoid(c)
    o_ref[...] = jnp.dot(sc, w_ref[...], preferred_element_type=F32,
                         precision=lax.Precision.HIGHEST) + b_ref[...]


def _ada_mod(c, w_ada, b_ada):
    bsz, d = c.shape
    n = w_ada.shape[1]
    tn = 1024
    return pl.pallas_call(
        _ada_kernel,
        grid=(n // tn,),
        in_specs=[pl.BlockSpec((bsz, d), lambda j: (0, 0)),
                  pl.BlockSpec((d, tn), lambda j: (0, j)),
                  pl.BlockSpec((1, tn), lambda j: (0, j))],
        out_specs=pl.BlockSpec((bsz, tn), lambda j: (0, j)),
        out_shape=jax.ShapeDtypeStruct((bsz, n), F32),
        compiler_params=_cparams(("parallel",)),
        name="ada_mod",
    )(c, w_ada, b_ada.reshape(1, n))


def _inproj_kernel(x_ref, g_ref, sc_ref, sh_ref, w_ref, b_ref, o_ref, h_ref):
    @pl.when(pl.program_id(2) == 0)
    def _():
        h_ref[...] = _rms_mod(x_ref[...], g_ref[...], sc_ref[...], sh_ref[...]).astype(BF16)

    o_ref[...] = jnp.dot(h_ref[...], w_ref[...], preferred_element_type=F32) + b_ref[...]


def _in_proj(x, g_mix, scale1, shift1, w_in_p, b_in_p):
    bsz, s, d = x.shape
    n = w_in_p.shape[1]
    tm = min(1024, s)
    tn = 512
    return pl.pallas_call(
        _inproj_kernel,
        grid=(bsz, s // tm, n // tn),
        in_specs=[pl.BlockSpec((None, tm, d), lambda b, i, j: (b, i, 0)),
                  pl.BlockSpec((1, d), lambda b, i, j: (0, 0)),
                  pl.BlockSpec((None, 1, d), lambda b, i, j: (b, 0, 0)),
                  pl.BlockSpec((None, 1, d), lambda b, i, j: (b, 0, 0)),
                  pl.BlockSpec((d, tn), lambda b, i, j: (0, j)),
                  pl.BlockSpec((1, tn), lambda b, i, j: (0, j))],
        out_specs=pl.BlockSpec((None, tm, tn), lambda b, i, j: (b, i, j)),
        out_shape=jax.ShapeDtypeStruct((bsz, s, n), F32),
        scratch_shapes=[pltpu.VMEM((tm, d), BF16)],
        compiler_params=_cparams(("parallel", "parallel", "arbitrary")),
        name="in_proj",
    )(x, g_mix.reshape(1, d), scale1, shift1, w_in_p, b_in_p.reshape(1, n))


def _rope_kernel(pos_ref, freq_ref, c_ref, s1_ref, s2_ref):
    ang = pos_ref[...] * freq_ref[...]
    cs = jnp.cos(ang)
    sn = jnp.sin(ang)
    d = lax.broadcasted_iota(jnp.int32, ang.shape, 1) & (HEAD_DIM - 1)
    c_ref[...] = cs
    s1_ref[...] = jnp.where(d < ROPE_DIMS // 2, -sn, 0.0)
    s2_ref[...] = jnp.where((d >= ROPE_DIMS // 2) & (d < ROPE_DIMS), sn, 0.0)


def _rope_tables(positions):
    bsz, s = positions.shape
    t = bsz * s
    tm = min(1024, t)
    inv_freq = ROPE_THETA ** (-(jnp.arange(0, ROPE_DIMS, 2, dtype=F32) / ROPE_DIMS))
    d = jnp.arange(LANES) % HEAD_DIM
    freq = jnp.where(d < ROPE_DIMS, inv_freq[d % (ROPE_DIMS // 2)], 0.0).astype(F32).reshape(1, LANES)
    pos = positions.astype(F32).reshape(t, 1)
    sds = jax.ShapeDtypeStruct((t, LANES), F32)
    outs = pl.pallas_call(
        _rope_kernel,
        grid=(t // tm,),
        in_specs=[pl.BlockSpec((tm, 1), lambda i: (i, 0)),
                  pl.BlockSpec((1, LANES), lambda i: (0, 0))],
        out_specs=[pl.BlockSpec((tm, LANES), lambda i: (i, 0))] * 3,
        out_shape=[sds, sds, sds],
        compiler_params=_cparams(("parallel",)),
        name="rope_tables",
    )(pos, freq)
    return [o.reshape(bsz, s, LANES) for o in outs]


def _rope_apply(t, c, s1, s2):
    outs = []
    for j in range(t.shape[1] // LANES):
        tj = t[:, j * LANES:(j + 1) * LANES]
        outs.append(tj * c + pltpu.roll(tj, LANES - ROPE_DIMS // 2, 1) * s1
                    + pltpu.roll(tj, ROPE_DIMS // 2, 1) * s2)
    return jnp.concatenate(outs, axis=1)


def _attn_kernel(sink_ref, q_ref, kvc_ref, kvp_ref, cq_ref, s1q_ref, s2q_ref,
                 cp_ref, s1p_ref, s2p_ref, bm_ref, o_ref):
    n = pl.program_id(1)
    cq, s1q, s2q = cq_ref[...], s1q_ref[...], s2q_ref[...]
    q = (_rope_apply(q_ref[...], cq, s1q, s2q) * (HEAD_DIM ** -0.5)).astype(BF16)
    kvc = kvc_ref[...]
    kvp = kvp_ref[...]
    kc = _rope_apply(kvc[:, :KV_WIDTH], cq, s1q, s2q)
    kp = _rope_apply(kvp[:, :KV_WIDTH], cp_ref[...], s1p_ref[...], s2p_ref[...])
    kband = jnp.concatenate([kp, kc], axis=0).astype(BF16)
    vband = jnp.concatenate([kvp[:, KV_WIDTH:], kvc[:, KV_WIDTH:]], axis=0).astype(BF16)
    bm = bm_ref[...]
    kexp = jnp.concatenate([kband] * N_KV_HEADS, axis=0) * bm
    vexp = jnp.concatenate([vband] * N_KV_HEADS, axis=0) * bm
    row = lax.broadcasted_iota(jnp.int32, (WINDOW, 2 * WINDOW), 0)
    col = lax.broadcasted_iota(jnp.int32, (WINDOW, 2 * WINDOW), 1)
    diff = row + WINDOW - col
    valid = (diff >= 0) & (diff < WINDOW) & ((col >= WINDOW) | (n > 0))
    rep = N_Q_HEADS // N_KV_HEADS
    kw = 2 * WINDOW
    for r in range(rep):
        qr = q[:, r * KV_WIDTH:(r + 1) * KV_WIDTH]
        sc = lax.dot_general(qr, kexp, (((1,), (1,)), ((), ())), preferred_element_type=F32)
        ps = []
        for g in range(N_KV_HEADS):
            s = jnp.where(valid, sc[:, g * kw:(g + 1) * kw], MASK_VALUE)
            sink = sink_ref[g * rep + r]
            m = jnp.maximum(jnp.max(s, axis=-1, keepdims=True), sink)
            p = jnp.exp(s - m)
            l = jnp.sum(p, axis=-1, keepdims=True) + jnp.exp(sink - m)
            ps.append((p / l).astype(BF16))
        pcat = jnp.concatenate(ps, axis=1)
        o = jnp.dot(pcat, vexp, preferred_element_type=F32)
        o_ref[:, r * KV_WIDTH:(r + 1) * KV_WIDTH] = o.astype(BF16)


def _attention(proj, sinks, ctab, s1tab, s2tab):
    bsz, s, _ = proj.shape
    nb = s // WINDOW
    g_row = jnp.arange(N_KV_HEADS * 2 * WINDOW) // (2 * WINDOW)
    g_col = jnp.arange(KV_WIDTH) // HEAD_DIM
    bm = (g_row[:, None] == g_col[None, :]).astype(BF16)
    cur = lambda b, n: (b, n, 0)
    prev = lambda b, n: (b, jnp.maximum(n - 1, 0), 0)
    kvw = 2 * KV_WIDTH
    tab = pl.BlockSpec((None, WINDOW, LANES), cur)
    tabp = pl.BlockSpec((None, WINDOW, LANES), prev)
    return pl.pallas_call(
        _attn_kernel,
        grid=(bsz, nb),
        in_specs=[pl.BlockSpec(memory_space=pltpu.SMEM),
                  pl.BlockSpec((None, WINDOW, ATTN_WIDTH), cur),
                  pl.BlockSpec((None, WINDOW, kvw), lambda b, n: (b, n, COL_KV // kvw)),
                  pl.BlockSpec((None, WINDOW, kvw), lambda b, n: (b, jnp.maximum(n - 1, 0), COL_KV // kvw)),
                  tab, tab, tab, tabp, tabp, tabp,
                  pl.BlockSpec((N_KV_HEADS * 2 * WINDOW, KV_WIDTH), lambda b, n: (0, 0))],
        out_specs=pl.BlockSpec((None, WINDOW, ATTN_WIDTH), cur),
        out_shape=jax.ShapeDtypeStruct((bsz, s, ATTN_WIDTH), BF16),
        compiler_params=_cparams(("parallel", "parallel")),
        name="attention",
    )(sinks, proj, proj, proj, ctab, s1tab, s2tab, ctab, s1tab, s2tab, bm)


def _ssm_prep_kernel(are_ref, aim_ref, ldt_ref, bre_ref, bim_ref,
                     abre_ref, abim_ref, bbre_ref, bbim_ref):
    dt = jnp.exp(ldt_ref[...])
    lam_re = jnp.minimum(are_ref[...], -1e-4)
    lam_im = aim_ref[...]
    mag = jnp.exp(lam_re * dt)
    ab_re = mag * jnp.cos(lam_im * dt)
    ab_im = mag * jnp.sin(lam_im * dt)
    den = lam_re * lam_re + lam_im * lam_im
    num_re = ab_re - 1.0
    z_re = (num_re * lam_re + ab_im * lam_im) / den
    z_im = (ab_im * lam_re - num_re * lam_im) / den
    abre_ref[...] = ab_re
    abim_ref[...] = ab_im
    bre = bre_ref[...]
    bim = bim_ref[...]
    bbre_ref[...] = z_re * bre - z_im * bim
    bbim_ref[...] = z_re * bim + z_im * bre


def _ssm_prep(a_re, a_im, log_dt, b_re, b_im):
    g, p, h = b_re.shape
    a3 = jax.ShapeDtypeStruct((g, 1, p), F32)
    b3 = jax.ShapeDtypeStruct((g, h, p), F32)
    return pl.pallas_call(
        _ssm_prep_kernel,
        out_shape=[a3, a3, b3, b3],
        name="ssm_prep",
    )(a_re.reshape(g, 1, p), a_im.reshape(g, 1, p), log_dt.reshape(g, 1, 1),
      jnp.swapaxes(b_re, 1, 2), jnp.swapaxes(b_im, 1, 2))


def _ssm_kernel(u_ref, wb_ref, wc_ref, are_ref, aim_ref, d_ref, wg_ref, bg_ref, o_ref,
                xre, xim, sre, sim, *, n_batch, t_chunk, lane_chunk):
    @pl.when(pl.program_id(0) == 0)
    def _():
        sre[...] = jnp.zeros_like(sre)
        sim[...] = jnp.zeros_like(sim)

    n_slices = SSM_WIDTH // LANES
    sw = SSM_SLICE_GROUPS * SSM_STATE
    u = u_ref[...]
    ub = u.astype(BF16)
    for s in range(n_slices):
        bu = jnp.dot(ub[:, s * LANES:(s + 1) * LANES], wb_ref[s], preferred_element_type=F32)
        xre[:, s * sw:(s + 1) * sw] = bu[:, :sw]
        xim[:, s * sw:(s + 1) * sw] = bu[:, sw:]

    for lc in range(STATE_LANES // lane_chunk):
        sl = slice(lc * lane_chunk, (lc + 1) * lane_chunk)
        ar = jnp.broadcast_to(are_ref[:, sl], (n_batch, lane_chunk))
        ai = jnp.broadcast_to(aim_ref[:, sl], (n_batch, lane_chunk))

        def body(t, carry, sl=sl, ar=ar, ai=ai):
            cr, ci = carry
            off = pl.multiple_of(t * n_batch, n_batch)
            nr = ar * cr - ai * ci + xre[pl.ds(off, n_batch), sl]
            ni = ar * ci + ai * cr + xim[pl.ds(off, n_batch), sl]
            xre[pl.ds(off, n_batch), sl] = nr
            xim[pl.ds(off, n_batch), sl] = ni
            return nr, ni

        cr, ci = lax.fori_loop(0, t_chunk, body, (sre[:, sl], sim[:, sl]), unroll=8)
        sre[:, sl] = cr
        sim[:, sl] = ci

    ys = []
    for s in range(n_slices):
        xs = jnp.concatenate([xre[:, s * sw:(s + 1) * sw], xim[:, s * sw:(s + 1) * sw]], axis=1)
        ys.append(jnp.dot(xs.astype(BF16), wc_ref[s], preferred_element_type=F32))
    y = jnp.concatenate(ys, axis=1) + d_ref[...] * u
    z = jax.nn.gelu(y)
    gl = jnp.dot(z.astype(BF16), wg_ref[...], preferred_element_type=F32) + bg_ref[...]
    o_ref[...] = (z * jax.nn.sigmoid(gl)).astype(BF16)


def _ssm(u_tm, n_batch, wb, wc, ab_re, ab_im, d_skip, w_glu, b_glu):
    rows, w = u_tm.shape
    s = rows // n_batch
    t_chunk = min(64, s)
    r = t_chunk * n_batch
    full = lambda shape: pl.BlockSpec(shape, lambda i: (0,) * len(shape))
    kern = functools.partial(_ssm_kernel, n_batch=n_batch, t_chunk=t_chunk, lane_chunk=512)
    return pl.pallas_call(
        kern,
        grid=(s // t_chunk,),
        in_specs=[pl.BlockSpec((r, w), lambda i: (i, 0)),
                  full(wb.shape), full(wc.shape), full((1, STATE_LANES)), full((1, STATE_LANES)),
                  full((1, w)), full((w, w)), full((1, w))],
        out_specs=pl.BlockSpec((r, w), lambda i: (i, 0)),
        out_shape=jax.ShapeDtypeStruct((rows, w), BF16),
        scratch_shapes=[pltpu.VMEM((r, STATE_LANES), F32), pltpu.VMEM((r, STATE_LANES), F32),
                        pltpu.VMEM((n_batch, STATE_LANES), F32), pltpu.VMEM((n_batch, STATE_LANES), F32)],
        compiler_params=_cparams(("arbitrary",)),
        name="ssm",
    )(u_tm, wb, wc, ab_re, ab_im, d_skip.reshape(1, w), w_glu, b_glu.reshape(1, w))


def _merge_kernel(attn_ref, sso_ref, ga_ref, gs_ref, x_ref, g1_ref, gf_ref, sc2_ref, sh2_ref,
                  wab_ref, wsb_ref, wo_ref, x1_ref, hi_ref, lo_ref):
    ya = jnp.dot(attn_ref[...], wab_ref[...], preferred_element_type=F32)
    ys = jnp.dot(sso_ref[...], wsb_ref[...], preferred_element_type=F32)
    merged = jax.nn.sigmoid(ga_ref[...]) * ya + jax.nn.sigmoid(gs_ref[...]) * ys
    mo = jnp.dot(merged.astype(BF16), wo_ref[...], preferred_element_type=F32)
    x1 = x_ref[...] + g1_ref[...] * mo
    x1_ref[...] = x1
    h2t = _rms_mod(x1, gf_ref[...], sc2_ref[...], sh2_ref[...]).T
    hi = h2t.astype(BF16)
    hi_ref[...] = hi
    lo_ref[...] = (h2t - hi.astype(F32)).astype(BF16)


def _merge(attn, sso, proj, x, gate1, g_ffn, scale2, shift2, wab, wsb, wo):
    bsz, s, d = x.shape
    tm = min(256, s)
    nt = s // tm
    row = lambda w, cb: pl.BlockSpec((None, tm, w), lambda b, i: (b, i, cb))
    per_b = pl.BlockSpec((None, 1, d), lambda b, i: (b, 0, 0))
    const = lambda shape: pl.BlockSpec(shape, lambda b, i: (0, 0), pipeline_mode=pl.Buffered(1))
    tsds = jax.ShapeDtypeStruct((d, bsz * s), BF16)
    tspec = pl.BlockSpec((d, tm), lambda b, i: (0, b * nt + i))
    return pl.pallas_call(
        _merge_kernel,
        grid=(bsz, nt),
        in_specs=[row(ATTN_WIDTH, 0), row(SSM_WIDTH, 0), row(d, COL_GA // d), row(d, COL_GS // d),
                  row(d, 0), per_b, pl.BlockSpec((1, d), lambda b, i: (0, 0)), per_b, per_b,
                  const(wab.shape), const(wsb.shape), const(wo.shape)],
        out_specs=[row(d, 0), tspec, tspec],
        out_shape=[jax.ShapeDtypeStruct((bsz, s, d), F32), tsds, tsds],
        compiler_params=_cparams(("parallel", "parallel")),
        name="merge",
    )(attn, sso, proj, proj, x, gate1, g_ffn.reshape(1, d), scale2, shift2, wab, wsb, wo)


def _dot3(a_hi, a_lo, b_hi, b_lo):
    return (jnp.dot(a_hi, b_hi, preferred_element_type=F32)
            + jnp.dot(a_hi, b_lo, preferred_element_type=F32)
            + jnp.dot(a_lo, b_hi, preferred_element_type=F32))


def _top16(v):
    idx = lax.broadcasted_iota(jnp.int32, v.shape, 0)
    rank = jnp.full(v.shape, 100.0, F32)
    vals = []
    for it in range(PEER_TOPK):
        m = jnp.max(v, axis=0, keepdims=True)
        istar = jnp.min(jnp.where(v == m, idx, PEER_NKEYS), axis=0, keepdims=True)
        sel = idx == istar
        rank = jnp.where(sel, float(it + 1), rank)
        v = jnp.where(sel, -jnp.inf, v)
        vals.append(m)
    return jnp.concatenate(vals, axis=0), rank


def _staircase(a, b):
    lanes = a.shape[1]
    rowi = lax.broadcasted_iota(jnp.int32, (8, lanes), 0)
    cs, codes = [], []
    for q in range(PEER_TOPK):
        nq = PEER_TOPK // (q + 1)
        for blk in range((nq + 7) // 8):
            p = rowi + 8 * blk
            c = a[8 * blk:8 * blk + 8, :] + b[q:q + 1, :]
            cs.append(jnp.where(p < nq, c, -jnp.inf))
            codes.append(p * PEER_TOPK + q)
    cand = jnp.concatenate(cs, axis=0)
    code = jnp.concatenate(codes, axis=0)
    row16 = lax.broadcasted_iota(jnp.int32, (PEER_TOPK, lanes), 0)
    cnt = jnp.zeros((PEER_TOPK, lanes), F32)
    m0 = a[0:1, :] + b[0:1, :]
    z = jnp.zeros((1, lanes), F32)
    for _ in range(PEER_TOPK):
        m = jnp.max(cand, axis=0, keepdims=True)
        cstar = jnp.min(jnp.where(cand == m, code, 1 << 20), axis=0, keepdims=True)
        cand = jnp.where(code == cstar, -jnp.inf, cand)
        z = z + jnp.exp(m - m0)
        cnt = cnt + jnp.where(row16 == (cstar >> 4), 1.0, 0.0)
    return cnt, z


def _batcher_pairs(n):
    pairs = []

    def merge(lo, hi, r):
        step = r * 2
        if step < hi - lo:
            merge(lo, hi, step)
            merge(lo + r, hi, step)
            pairs.extend((i, i + r) for i in range(lo + r, hi - r, step))
        else:
            pairs.append((lo, lo + r))

    def sort(lo, hi):
        if hi - lo >= 1:
            mid = lo + (hi - lo) // 2
            sort(lo, mid)
            sort(mid + 1, hi)
            merge(lo, hi, 1)

    sort(0, n - 1)
    return pairs


_SORT16 = _batcher_pairs(PEER_TOPK)
SUBLANES = 8


def _cmpx(v, i, j):
    hi, lo = jnp.maximum(v[i], v[j]), jnp.minimum(v[i], v[j])
    v[i], v[j] = hi, lo


def _sub_allreduce(x, op):
    for shift in (4, 2, 1):
        x = op(x, pltpu.roll(x, shift, 0))
    return x


def _sorted_top16(s):
    v = [s[SUBLANES * k:SUBLANES * (k + 1), :] for k in range(PEER_TOPK)]
    for i, j in _SORT16:
        _cmpx(v, i, j)
    for shift in (4, 2, 1):
        m = [jnp.maximum(v[k], pltpu.roll(v[PEER_TOPK - 1 - k], shift, 0)) for k in range(PEER_TOPK)]
        for d in (8, 4, 2, 1):
            for i in range(PEER_TOPK):
                if not i & d:
                    _cmpx(m, i, i + d)
        v = m
    return v


def _count_sorted(b, passes):
    t3 = passes(b[7])
    t2 = passes(jnp.where(t3, b[11], b[3]))
    t1 = passes(jnp.where(t3, jnp.where(t2, b[13], b[9]), jnp.where(t2, b[5], b[1])))
    c0 = jnp.where(t3,
                   jnp.where(t2, jnp.where(t1, b[14], b[12]), jnp.where(t1, b[10], b[8])),
                   jnp.where(t2, jnp.where(t1, b[6], b[4]), jnp.where(t1, b[2], b[0])))
    t0 = passes(c0)
    cnt = (jnp.where(t3, 8.0, 0.0) + jnp.where(t2, 4.0, 0.0)
           + jnp.where(t1, 2.0, 0.0) + jnp.where(t0, 1.0, 0.0))
    return jnp.where(passes(b[15]), 16.0, cnt)


def _route_fast(s0, s1):
    a = _sorted_top16(s0)
    b = _sorted_top16(s1)
    sub = lax.broadcasted_iota(jnp.int32, a[0].shape, 0)
    ninf = -jnp.inf

    def pick(vs, base):
        acc = vs[base + SUBLANES - 1]
        for r in range(SUBLANES - 2, -1, -1):
            acc = jnp.where(sub == r, vs[base + r], acc)
        return acc

    a_lo, a_hi, b_lo, b_hi = pick(a, 0), pick(a, 8), pick(b, 0), pick(b, 8)
    ge2 = sub >= 2
    packs = [a[0] + b_lo, a[0] + b_hi, a[1] + b_lo, a_hi + b[0],
             jnp.where(ge2, a_lo + b[0], ninf), jnp.where(ge2, a_lo + b[1], ninf),
             jnp.where(ge2 & (sub <= 4), a_lo + b[2], ninf),
             jnp.where((sub == 3) | (sub == 4), a[2] + b_lo, jnp.where(sub == 5, a[3] + b[3], ninf))]
    work = list(packs)
    tau = None
    for it in range(PEER_TOPK):
        m = work[0]
        for w in work[1:]:
            m = jnp.maximum(m, w)
        tau = _sub_allreduce(m, jnp.maximum)
        if it < PEER_TOPK - 1:
            work = [jnp.where(w == tau, ninf, w) for w in work]
    m0 = a[0] + b[0]
    z = None
    for p in packs:
        term = jnp.where(p >= tau, jnp.exp(p - m0), 0.0)
        z = term if z is None else z + term
    z = _sub_allreduce(z, jnp.add)

    g, l, e1, r1 = [], [], [], []
    lsum = c16 = rsum = None
    for k in range(PEER_NKEYS // SUBLANES):
        s0v = s0[SUBLANES * k:SUBLANES * (k + 1), :]
        s1v = s1[SUBLANES * k:SUBLANES * (k + 1), :]
        lv = _count_sorted(b, lambda x, s0v=s0v: (s0v + x) >= tau)
        rv = _count_sorted(b, lambda x, s1v=s1v: x > s1v) + 1.0
        in16 = rv <= float(PEER_TOPK)
        one = jnp.where(in16, 1.0, 0.0)
        rin = jnp.where(in16, rv, 0.0)
        lsum = lv if lsum is None else lsum + lv
        c16 = one if c16 is None else c16 + one
        rsum = rin if rsum is None else rsum + rin
        g.append(jnp.exp(s0v - a[0]) / z)
        l.append(lv)
        e1.append(jnp.exp(s1v - b[0]))
        r1.append(rv)
    lsum = _sub_allreduce(lsum, jnp.add)
    c16 = _sub_allreduce(c16, jnp.add)
    rsum = _sub_allreduce(rsum, jnp.add)
    n = float(PEER_TOPK)
    clean = (lsum == n) & (c16 == n) & (rsum == n * (n + 1.0) / 2.0)
    return g, l, e1, r1, jnp.where(clean, 0.0, 1.0)


def _route_exact(s0, s1):
    a, rank0 = _top16(s0)
    b, rank1 = _top16(s1)
    cnt, z = _staircase(a, b)
    lsel = jnp.zeros(s0.shape, F32)
    for p in range(PEER_TOPK):
        lsel = lsel + jnp.where(rank0 == float(p + 1), cnt[p:p + 1, :], 0.0)
    return jnp.exp(s0 - a[0:1, :]) / z, lsel, jnp.exp(s1 - b[0:1, :]), rank1


def _kw_kernel(kh_ref, kl_ref, wh_ref, wl_ref, o_ref):
    o_ref[...] = _dot3(kh_ref[...], kl_ref[...], wh_ref[...], wl_ref[...])


def _key_query_weights(k_hi, k_lo, wq_hi, wq_lo):
    n, nk, hd = k_hi.shape
    d = wq_hi.shape[-1]
    ks = pl.BlockSpec((None, nk, hd), lambda i: (i, 0, 0))
    ws = pl.BlockSpec((None, hd, d), lambda i: (i, 0, 0))
    return pl.pallas_call(
        _kw_kernel,
        grid=(n,),
        in_specs=[ks, ks, ws, ws],
        out_specs=pl.BlockSpec((None, nk, d), lambda i: (i, 0, 0)),
        out_shape=jax.ShapeDtypeStruct((n, nk, d), F32),
        compiler_params=_cparams(("parallel",)),
        name="peer_kw",
    )(k_hi, k_lo, wq_hi, wq_lo)


def _route_kernel(xh_ref, xl_ref, wh_ref, wl_ref, g_ref, l_ref, e1_ref, r1_ref, sc_ref):
    sc_ref[...] = _dot3(wh_ref[...], wl_ref[...], xh_ref[...], xl_ref[...])
    ng = sc_ref.shape[1] // LANES

    def lane_group(idx, carry):
        h = idx // ng
        off = pl.multiple_of((idx % ng) * LANES, LANES)
        row = pl.multiple_of(h * (2 * PEER_NKEYS), 2 * PEER_NKEYS)
        s0 = sc_ref[pl.ds(row, PEER_NKEYS), pl.ds(off, LANES)]
        s1 = sc_ref[pl.ds(row + PEER_NKEYS, PEER_NKEYS), pl.ds(off, LANES)]
        g, l, e1, r1, bad = _route_fast(s0, s1)
        cat = lambda pieces: jnp.concatenate(pieces, axis=0)
        g_ref[h, :, pl.ds(off, LANES)] = cat(g)
        l_ref[h, :, pl.ds(off, LANES)] = cat(l)
        e1_ref[h, :, pl.ds(off, LANES)] = cat(e1).astype(BF16)
        r1_ref[h, :, pl.ds(off, LANES)] = cat(r1).astype(BF16)

        @pl.when(jnp.max(bad) > 0.0)
        def _():
            gx, lx, e1x, r1x = _route_exact(s0, s1)
            g_ref[h, :, pl.ds(off, LANES)] = gx
            l_ref[h, :, pl.ds(off, LANES)] = lx
            e1_ref[h, :, pl.ds(off, LANES)] = e1x.astype(BF16)
            r1_ref[h, :, pl.ds(off, LANES)] = r1x.astype(BF16)

        return carry

    lax.fori_loop(0, PEER_HEADS * ng, lane_group, 0)


def _peer_route(h2t_hi, h2t_lo, kw_hi, kw_lo):
    d, t = h2t_hi.shape
    tn = min(256, t)
    nq = kw_hi.shape[0]
    xs = pl.BlockSpec((d, tn), lambda i: (0, i))
    ws = pl.BlockSpec((nq, d), lambda i: (0, 0), pipeline_mode=pl.Buffered(1))
    os_ = pl.BlockSpec((PEER_HEADS, PEER_NKEYS, tn), lambda i: (0, 0, i))
    f = jax.ShapeDtypeStruct((PEER_HEADS, PEER_NKEYS, t), F32)
    bf = jax.ShapeDtypeStruct((PEER_HEADS, PEER_NKEYS, t), BF16)
    return pl.pallas_call(
        _route_kernel,
        grid=(t // tn,),
        in_specs=[xs, xs, ws, ws],
        out_specs=[os_, os_, os_, os_],
        out_shape=[f, f, bf, bf],
        scratch_shapes=[pltpu.VMEM((nq, tn), F32)],
        compiler_params=_cparams(("parallel",)),
        name="peer_route",
    )(h2t_hi, h2t_lo, kw_hi, kw_lo)


def _dense_kernel(h_ref, dn_ref, up_ref, g_ref, l_ref, e1_ref, r1_ref, o_ref, *, n_i):
    @pl.when(pl.program_id(1) == 0)
    def _():
        o_ref[...] = jnp.zeros_like(o_ref)

    hx = h_ref[...]
    wgs = []
    ipc = 1
    for ic in range(n_i // ipc):
        rows = slice(ic * ipc * PEER_NKEYS, (ic + 1) * ipc * PEER_NKEYS)
        at = jnp.dot(dn_ref[rows, :], hx, preferred_element_type=F32)
        ws = []
        for il in range(ic * ipc, (ic + 1) * ipc):
            w = None
            for h in range(PEER_HEADS):
                gi = g_ref[h, il:il + 1, :].astype(BF16)
                li = l_ref[h, il:il + 1, :].astype(BF16)
                term = jnp.where(r1_ref[h] <= li, e1_ref[h] * gi, jnp.zeros((), BF16))
                w = term if w is None else w + term
            ws.append(w)
        wgs.append(jax.nn.gelu(at.astype(BF16)) * jnp.concatenate(ws, axis=0))
    wg = jnp.concatenate(wgs, axis=0)
    o_ref[...] += jnp.dot(up_ref[...], wg, preferred_element_type=F32)


def _peer_dense(h2t_hi, down_bf, up_t_bf, g, l, e1, r1):
    d, t = h2t_hi.shape
    ne = down_bf.shape[0]
    tm = min(512, t)
    en = 1024
    n_i = en // PEER_NKEYS
    rt = lambda n: pl.BlockSpec((PEER_HEADS, n, tm), lambda i, e: (0, e if n == n_i else 0, i))
    return pl.pallas_call(
        functools.partial(_dense_kernel, n_i=n_i),
        grid=(t // tm, ne // en),
        in_specs=[pl.BlockSpec((d, tm), lambda i, e: (0, i)),
                  pl.BlockSpec((en, d), lambda i, e: (e, 0)),
                  pl.BlockSpec((d, en), lambda i, e: (0, e)),
                  rt(n_i), rt(n_i), rt(PEER_NKEYS), rt(PEER_NKEYS)],
        out_specs=pl.BlockSpec((d, tm), lambda i, e: (0, i)),
        out_shape=jax.ShapeDtypeStruct((d, t), F32),
        compiler_params=_cparams(("parallel", "arbitrary")),
        name="peer_dense",
    )(h2t_hi, down_bf, up_t_bf, g, l, e1, r1)


def _final_kernel(x1_ref, pt_ref, g2_ref, gf_ref, o_ref):
    x2 = x1_ref[...] + g2_ref[...] * pt_ref[...].T
    ms = jnp.mean(x2 * x2, axis=-1, keepdims=True)
    o_ref[...] = x2 * lax.rsqrt(ms + RMS_EPS) * gf_ref[...]


def _final(x1, peer_t, gate2, g_final):
    bsz, s, d = x1.shape
    tm = min(512, s)
    nt = s // tm
    row = pl.BlockSpec((None, tm, d), lambda b, i: (b, i, 0))
    return pl.pallas_call(
        _final_kernel,
        grid=(bsz, nt),
        in_specs=[row, pl.BlockSpec((d, tm), lambda b, i: (0, b * nt + i)),
                  pl.BlockSpec((None, 1, d), lambda b, i: (b, 0, 0)),
                  pl.BlockSpec((1, d), lambda b, i: (0, 0))],
        out_specs=row,
        out_shape=jax.ShapeDtypeStruct((bsz, s, d), F32),
        compiler_params=_cparams(("parallel", "parallel")),
        name="final",
    )(x1, peer_t, gate2, g_final.reshape(1, d))


def _head_perm(w, axis):
    rep = N_Q_HEADS // N_KV_HEADS
    shape = w.shape
    w = w.reshape(shape[:axis] + (N_KV_HEADS, rep, HEAD_DIM) + shape[axis + 1:])
    return jnp.swapaxes(w, axis, axis + 1).reshape(shape)


def _split_bf16(w):
    hi = w.astype(BF16)
    return hi, (w - hi.astype(F32)).astype(BF16)


def _layer(x, c, positions, w_ada, b_ada, g_mix, w_in, b_in, attn_sinks, w_attn_branch,
           ssm_A_re, ssm_A_im, ssm_log_dt, ssm_B_re, ssm_B_im, ssm_C_re, ssm_C_im, ssm_D,
           w_glu, b_glu, w_ssm_branch, w_out, g_ffn, w_query, sub_keys, expert_down, expert_up):
    bsz, s, d = x.shape
    mod = _ada_mod(c, w_ada, b_ada)
    shift1, scale1, gate1, shift2, scale2, gate2 = [m.reshape(bsz, 1, d) for m in jnp.split(mod, 6, axis=-1)]

    o1, o3, o4 = ATTN_WIDTH, ATTN_WIDTH + 2 * KV_WIDTH, ATTN_WIDTH + 2 * KV_WIDTH + SSM_WIDTH
    permute = lambda w: jnp.concatenate(
        [_head_perm(w[..., :o1], w.ndim - 1), w[..., o3:o4], w[..., o4:], w[..., o1:o3]], axis=-1)
    proj = _in_proj(x, g_mix, scale1, shift1, permute(w_in).astype(BF16), permute(b_in))

    ctab, s1tab, s2tab = _rope_tables(positions)
    attn = _attention(proj, attn_sinks, ctab, s1tab, s2tab)

    ab_re, ab_im, bb_re, bb_im = _ssm_prep(ssm_A_re, ssm_A_im, ssm_log_dt, ssm_B_re, ssm_B_im)
    ns = N_SSM_GROUPS // SSM_SLICE_GROUPS
    eye = jnp.eye(SSM_SLICE_GROUPS, dtype=F32)
    shp = (ns, SSM_SLICE_GROUPS, SSM_GROUP, SSM_STATE)
    blk_b = lambda m: jnp.einsum("sghp,gk->sghkp", m.reshape(shp), eye).reshape(ns, LANES, -1)
    blk_c = lambda m: jnp.einsum("sghp,gk->sgpkh", m.reshape(shp), eye).reshape(ns, -1, LANES)
    wb = jnp.concatenate([blk_b(bb_re), blk_b(bb_im)], axis=-1).astype(BF16)
    wc = jnp.concatenate([blk_c(ssm_C_re), -blk_c(ssm_C_im)], axis=1).astype(BF16)
    u_tm = jnp.swapaxes(proj[:, :, COL_U:COL_U + SSM_WIDTH], 0, 1).reshape(s * bsz, SSM_WIDTH)
    sso_tm = _ssm(u_tm, bsz, wb, wc, ab_re.reshape(1, -1), ab_im.reshape(1, -1), ssm_D,
                  w_glu.astype(BF16), b_glu)
    sso = jnp.swapaxes(sso_tm.reshape(s, bsz, SSM_WIDTH), 0, 1)

    x1, h2t_hi, h2t_lo = _merge(attn, sso, proj, x, gate1, g_ffn, scale2, shift2,
                                _head_perm(w_attn_branch, 0).astype(BF16), w_ssm_branch.astype(BF16),
                                w_out.astype(BF16))

    wq_hi, wq_lo = _split_bf16(w_query.T.reshape(2 * PEER_HEADS, PEER_HALF, d))
    k_hi, k_lo = _split_bf16(sub_keys.reshape(2 * PEER_HEADS, PEER_NKEYS, PEER_HALF))
    kw_hi, kw_lo = _split_bf16(_key_query_weights(k_hi, k_lo, wq_hi, wq_lo).reshape(-1, d))
    g, l, e1, r1 = _peer_route(h2t_hi, h2t_lo, kw_hi, kw_lo)
    peer_t = _peer_dense(h2t_hi, expert_down.astype(BF16), expert_up.T.astype(BF16), g, l, e1, r1)
    return x1, peer_t, gate2


def kernel(x, c, positions, w_ada, b_ada, g_mix, w_in, b_in, attn_sinks, w_attn_branch, ssm_A_re, ssm_A_im, ssm_log_dt, ssm_B_re, ssm_B_im, ssm_C_re, ssm_C_im, ssm_D, w_glu, b_glu, w_ssm_branch, w_out, g_ffn, w_query, sub_keys, expert_down, expert_up, g_final):
    depth = w_ada.shape[0]
    assert depth == 1
    x1, peer_t, gate2 = _layer(x, c, positions, w_ada[0], b_ada[0], g_mix[0], w_in[0], b_in[0],
                               attn_sinks[0], w_attn_branch[0], ssm_A_re[0], ssm_A_im[0], ssm_log_dt[0],
                               ssm_B_re[0], ssm_B_im[0], ssm_C_re[0], ssm_C_im[0], ssm_D[0], w_glu[0],
                               b_glu[0], w_ssm_branch[0], w_out[0], g_ffn[0], w_query[0], sub_keys[0],
                               expert_down[0], expert_up[0])
    return _final(x1, peer_t, gate2, g_final)
```
